```python
import math
import jax, jax.numpy as jnp
from jax import lax
import numpy as np

D_MODEL = 1024
BATCH = 16
SEQ = 4096
DEPTH = 2

N_A = DEPTH // 2
N_B = DEPTH - N_A

E_A = D_MODEL
CONV_WIDTH = 3
CONV_GROUPS = 16

N_HEADS = 8
QK_NOPE = 64
QK_ROPE = 32
V_HEAD = 64
KV_RANK = 256
Q_RANK = 384
E_B = N_HEADS * V_HEAD
ROPE_THETA = 10000.0
Q_BLOCK = 128
SOFTMAX_SCALE = 1.0 / math.sqrt(QK_NOPE + QK_ROPE)

EPS = 1e-6

kernel_name = "yoco_shortconv_mla_hybrid"


def rms_norm(x, g):
    xf = x.astype(jnp.float32)
    y = xf * lax.rsqrt(jnp.mean(xf * xf, axis=-1, keepdims=True) + EPS)
    return (y * g.astype(jnp.float32)).astype(x.dtype)


def rope_tables(positions, dtype):
    inv_freq = ROPE_THETA ** (-jnp.arange(0, QK_ROPE, 2, dtype=jnp.float32) / QK_ROPE)
    ang = positions.astype(jnp.float32)[..., None] * inv_freq
    return jnp.cos(ang).astype(dtype), jnp.sin(ang).astype(dtype)


def apply_rope(x, cos, sin):
    x1, x2 = jnp.split(x, 2, axis=-1)
    return jnp.concatenate([x1 * cos - x2 * sin, x1 * sin + x2 * cos], axis=-1)


def conv_mixer(x, norm_g, w_in, conv_w, w_out):
    S = x.shape[1]
    h = rms_norm(x, norm_g)
    proj = h @ w_in
    b, c, u, g = jnp.split(proj, 4, axis=-1)
    v = c * u
    vp = jnp.pad(v, ((0, 0), (CONV_WIDTH - 1, 0), (0, 0)))
    conv = conv_w[0] * vp[:, 0:S]
    for k in range(1, CONV_WIDTH):
        conv = conv + conv_w[k] * vp[:, k:k + S]
    y = jax.nn.silu(g) * b * conv
    return y @ w_out


def shared_kv(s, kv_norm, w_dkv, ckv_norm, w_ukv, cos, sin):
    Bsz, S, _ = s.shape
    h = rms_norm(s, kv_norm)
    ckr = h @ w_dkv
    c_kv, k_rope = ckr[..., :KV_RANK], ckr[..., KV_RANK:]
    c_kv = rms_norm(c_kv, ckv_norm)
    kv = (c_kv @ w_ukv).reshape(Bsz, S, N_HEADS, QK_NOPE + V_HEAD)
    k_nope, v = kv[..., :QK_NOPE], kv[..., QK_NOPE:]
    k_rope = apply_rope(k_rope, cos, sin)
    return k_nope, k_rope, v


def causal_attention(q_nope, q_rope, k_nope, k_rope, v):
    Bsz, S, H, _ = q_nope.shape
    nb = S // Q_BLOCK
    kpos = jnp.arange(S)

    def one_block(args):
        qn, qr, i = args
        sc = (jnp.einsum('bqhd,bkhd->bhqk', qn, k_nope)
              + jnp.einsum('bqhr,bkr->bhqk', qr, k_rope))
        sc = sc.astype(jnp.float32) * SOFTMAX_SCALE
        qpos = i * Q_BLOCK + jnp.arange(Q_BLOCK)
        mask = kpos[None, :] <= qpos[:, None]
        sc = jnp.where(mask[None, None], sc, -jnp.inf)
        p = jax.nn.softmax(sc, axis=-1).astype(v.dtype)
        return jnp.einsum('bhqk,bkhv->bqhv', p, v)

    qn_b = q_nope.reshape(Bsz, nb, Q_BLOCK, H, QK_NOPE).transpose(1, 0, 2, 3, 4)
    qr_b = q_rope.reshape(Bsz, nb, Q_BLOCK, H, QK_ROPE).transpose(1, 0, 2, 3, 4)
    out = lax.map(one_block, (qn_b, qr_b, jnp.arange(nb)))
    return out.transpose(1, 0, 2, 3, 4).reshape(Bsz, S, H, V_HEAD)


def mla_layer(x, norm_g, w_in, q_norm, w_uq, w_out, k_nope, k_rope, v, cos, sin):
    Bsz, S, _ = x.shape
    h = rms_norm(x, norm_g)
    proj = h @ w_in
    c_q, g = proj[..., :Q_RANK], proj[..., Q_RANK:]
    q = (rms_norm(c_q, q_norm) @ w_uq).reshape(Bsz, S, N_HEADS, QK_NOPE + QK_ROPE)
    q_nope, q_rope = q[..., :QK_NOPE], q[..., QK_NOPE:]
    q_rope = apply_rope(q_rope, cos[:, :, None, :], sin[:, :, None, :])
    o = causal_attention(q_nope, q_rope, k_nope, k_rope, v).reshape(Bsz, S, E_B)
    return (o * jax.nn.silu(g)) @ w_out


def setup_inputs(seed: int = 0) -> dict:
    key = jax.random.key(seed)
    ks = jax.random.split(key, 16)
    f32 = jnp.float32

    def w(k, shape, fan_in):
        return jax.random.normal(k, shape, f32) * (fan_in ** -0.5)

    def gain(k, shape):
        return 1.0 + 0.1 * jax.random.normal(k, shape, f32)

    x = jax.random.normal(ks[0], (BATCH, SEQ, D_MODEL), f32)
    positions = jnp.broadcast_to(jnp.arange(SEQ, dtype=jnp.int32), (BATCH, SEQ))
    return {
        "x": x,
        "positions": positions,
        "a_norm": gain(ks[1], (N_A, D_MODEL)),
        "a_w_in": w(ks[2], (N_A, D_MODEL, 4 * E_A), D_MODEL),
        "a_conv": w(ks[3], (N_A, CONV_WIDTH, E_A), CONV_WIDTH),
        "a_w_out": w(ks[4], (N_A, E_A, D_MODEL), E_A),
        "kv_norm": gain(ks[5], (D_MODEL,)),
        "w_dkv": w(ks[6], (D_MODEL, KV_RANK + QK_ROPE), D_MODEL),
        "ckv_norm": gain(ks[7], (KV_RANK,)),
        "w_ukv": w(ks[8], (KV_RANK, N_HEADS * (QK_NOPE + V_HEAD)), KV_RANK),
        "b_norm": gain(ks[9], (N_B, D_MODEL)),
        "b_w_in": w(ks[10], (N_B, D_MODEL, Q_RANK + E_B), D_MODEL),
        "b_q_norm": gain(ks[11], (N_B, Q_RANK)),
        "b_w_uq": w(ks[12], (N_B, Q_RANK, N_HEADS * (QK_NOPE + QK_ROPE)), Q_RANK),
        "b_w_out": w(ks[13], (N_B, E_B, D_MODEL), E_B),
        "final_norm": gain(ks[14], (D_MODEL,)),
    }


def reference(x, positions, a_norm, a_w_in, a_conv, a_w_out, kv_norm, w_dkv, ckv_norm,
              w_ukv, b_norm, b_w_in, b_q_norm, b_w_uq, b_w_out, final_norm):
    cos, sin = rope_tables(positions, x.dtype)
    k_nope = k_rope = v = None
    for layer in range(DEPTH):
        if layer < N_A:
            x = x + conv_mixer(x, a_norm[layer], a_w_in[layer], a_conv[layer], a_w_out[layer])
            if layer == N_A - 1:
                k_nope, k_rope, v = shared_kv(x, kv_norm, w_dkv, ckv_norm, w_ukv, cos, sin)
        else:
            j = layer - N_A
            x = x + mla_layer(x, b_norm[j], b_w_in[j], b_q_norm[j], b_w_uq[j], b_w_out[j],
                              k_nope, k_rope, v, cos, sin)
    return rms_norm(x, final_norm)
```

```python
import math
from functools import partial

import jax
import jax.numpy as jnp
from jax import lax
from jax.experimental import pallas as pl
from jax.experimental.pallas import tpu as pltpu

D_MODEL = 1024
E_A = D_MODEL
CONV_WIDTH = 3
N_HEADS = 8
QK_NOPE = 64
QK_ROPE = 32
V_HEAD = 64
KV_RANK = 256
Q_RANK = 384
E_B = N_HEADS * V_HEAD
ROPE_THETA = 10000.0
SOFTMAX_SCALE = 1.0 / math.sqrt(QK_NOPE + QK_ROPE)
EPS = 1e-6

LANES = 128
HEAD_PACK = LANES
TOKEN_TILE = 512
CONV_CHUNK = 256
N_CONV_CHUNKS = E_A // CONV_CHUNK
CARRY_ROWS = 8
ATT_BLOCK = 512
HEADS_PER_STEP = 2
MASK_VALUE = -1e30
VMEM_LIMIT_BYTES = 56 * 1024 * 1024

BF16 = jnp.bfloat16
F32 = jnp.float32


def _rms_scale(x):
    return lax.rsqrt(jnp.mean(x * x, axis=-1, keepdims=True) + EPS)


def _dot(a, b):
    return jnp.dot(a, b, preferred_element_type=F32)


def _token_kernel(x_ref, cc_ref, ss_ref, anorm_ref, win_ref, conv_ref, wouta_ref,
                  kvnorm_ref, wdkv_ref, ckvnorm_ref, wukv_ref, bnorm_ref, wcq_ref,
                  qnorm_ref, wq_ref,
                  x1_ref, q_ref, k_ref, v_ref,
                  vbuf_ref, y_ref):
    tm = TOKEN_TILE
    cw = CONV_CHUNK

    @pl.when(pl.program_id(1) == 0)
    def _():
        vbuf_ref[0:CARRY_ROWS, :] = jnp.zeros((CARRY_ROWS, E_A), F32)

    x = x_ref[...]
    h = (x * _rms_scale(x) * anorm_ref[...]).astype(BF16)

    for j in range(N_CONV_CHUNKS):
        cols = slice(j * cw, (j + 1) * cw)
        pj = _dot(h, win_ref[:, j * 4 * cw:(j + 1) * 4 * cw])
        b = pj[:, 0:cw]
        c = pj[:, cw:2 * cw]
        u = pj[:, 2 * cw:3 * cw]
        g = pj[:, 3 * cw:4 * cw]
        v = c * u
        vbuf_ref[CARRY_ROWS:CARRY_ROWS + tm, cols] = v
        vm1 = vbuf_ref[CARRY_ROWS - 1:CARRY_ROWS - 1 + tm, cols]
        vm2 = vbuf_ref[CARRY_ROWS - 2:CARRY_ROWS - 2 + tm, cols]
        conv = conv_ref[0:1, cols] * vm2 + conv_ref[1:2, cols] * vm1 + conv_ref[2:3, cols] * v
        y = (g * jax.nn.sigmoid(g)) * b * conv
        y_ref[:, cols] = y.astype(BF16)

    vbuf_ref[0:CARRY_ROWS, :] = vbuf_ref[tm:tm + CARRY_ROWS, :]

    x1 = x + _dot(y_ref[...], wouta_ref[...])
    x1_ref[...] = x1

    xn = x1 * _rms_scale(x1)
    hkv = (xn * kvnorm_ref[...]).astype(BF16)
    hb = (xn * bnorm_ref[...]).astype(BF16)

    cc = cc_ref[...]
    ss = ss_ref[...]

    ckr = _dot(hkv, wdkv_ref[...])
    ckv = ckr[:, 0:KV_RANK]
    rope_k = ckr[:, KV_RANK:KV_RANK + LANES] * cc + ckr[:, KV_RANK + LANES:KV_RANK + 2 * LANES] * ss
    ckvn = (ckv * _rms_scale(ckv) * ckvnorm_ref[...]).astype(BF16)
    kv = _dot(ckvn, wukv_ref[...])

    lane = lax.broadcasted_iota(jnp.int32, (1, LANES), 1)
    ones_col = jnp.where(lane == V_HEAD, 1.0, 0.0).astype(F32)

    cq = _dot(hb, wcq_ref[...])
    cqn = (cq * _rms_scale(cq) * qnorm_ref[...]).astype(BF16)
    q = _dot(cqn, wq_ref[...])
    qscale = SOFTMAX_SCALE * math.log2(math.e)
    tq = jnp.where(lane < QK_NOPE, 1.0, jnp.where(lane < QK_NOPE + QK_ROPE, cc, ss)) * qscale

    for hd in range(N_HEADS):
        hc = slice(hd * HEAD_PACK, (hd + 1) * HEAD_PACK)
        vc = slice((N_HEADS + hd) * HEAD_PACK, (N_HEADS + hd + 1) * HEAD_PACK)
        k_ref[:, hc] = (kv[:, hc] + rope_k).astype(BF16)
        v_ref[:, hc] = (kv[:, vc] + ones_col).astype(BF16)
        q_ref[:, hc] = (q[:, hc] * tq).astype(BF16)


def _attention_kernel(q_ref, k_ref, v_ref, o_ref, m_ref, acc_ref):
    blk = ATT_BLOCK
    nblk = q_ref.shape[0] // blk

    def update(qi, ki, masked):
        qrows = pl.ds(pl.multiple_of(qi * blk, blk), blk)
        krows = pl.ds(pl.multiple_of(ki * blk, blk), blk)
        for hd in range(HEADS_PER_STEP):
            hc = slice(hd * HEAD_PACK, (hd + 1) * HEAD_PACK)
            q = q_ref[qrows, hc]
            k = k_ref[krows, hc]
            v = v_ref[krows, hc]
            s = lax.dot_general(q, k, (((1,), (1,)), ((), ())), preferred_element_type=F32)
            if masked:
                row = lax.broadcasted_iota(jnp.int32, (blk, blk), 0)
                col = lax.broadcasted_iota(jnp.int32, (blk, blk), 1)
                s = jnp.where(col <= row, s, MASK_VALUE)
            m_old = m_ref[hd]
            m_new = jnp.maximum(m_old, jnp.max(s, axis=-1, keepdims=True))
            p = jnp.exp2(s - m_new)
            alpha = jnp.exp2(m_old - m_new)
            acc_ref[hd] = alpha * acc_ref[hd] + _dot(p.astype(BF16), v)
            m_ref[hd] = m_new

    def q_block(qi, carry):
        m_ref[...] = jnp.full(m_ref.shape, MASK_VALUE, F32)
        acc_ref[...] = jnp.zeros(acc_ref.shape, F32)

        def kv_block(ki, c):
            update(qi, ki, False)
            return c

        lax.fori_loop(0, qi, kv_block, 0)
        update(qi, qi, True)

        qrows = pl.ds(pl.multiple_of(qi * blk, blk), blk)
        for hd in range(HEADS_PER_STEP):
            acc = acc_ref[hd]
            o = acc[:, 0:V_HEAD] / acc[:, V_HEAD:V_HEAD + 1]
            o_ref[qrows, hd * V_HEAD:(hd + 1) * V_HEAD] = o.astype(o_ref.dtype)
        return carry

    lax.fori_loop(0, nblk, q_block, 0)


def _output_kernel(x1_ref, o_ref, bnorm_ref, wg_ref, woutb_ref, fnorm_ref, out_ref):
    x1 = x1_ref[...]
    hb = (x1 * _rms_scale(x1) * bnorm_ref[...]).astype(BF16)
    g = _dot(hb, wg_ref[...])
    y = (o_ref[...].astype(F32) * (g * jax.nn.sigmoid(g))).astype(BF16)
    x2 = x1 + _dot(y, woutb_ref[...])
    out_ref[...] = x2 * _rms_scale(x2) * fnorm_ref[...]


def _const_spec(shape):
    return pl.BlockSpec(shape, lambda *_: (0,) * len(shape))


def _rope_lane_tables(positions):
    inv_freq = ROPE_THETA ** (-jnp.arange(0, QK_ROPE, 2, dtype=F32) / QK_ROPE)
    ang = positions.astype(F32)[..., None] * inv_freq
    cos, sin = jnp.cos(ang), jnp.sin(ang)
    reps = LANES // QK_ROPE
    cc = jnp.tile(jnp.concatenate([cos, cos], axis=-1), (1, 1, reps))
    ss = jnp.tile(jnp.concatenate([-sin, sin], axis=-1), (1, 1, reps))
    return cc, ss


def _swap_halves(w):
    half = w.shape[-1] // 2
    return jnp.concatenate([w[..., half:], w[..., :half]], axis=-1)


def _prepare_weights(a_w_in, w_dkv, w_ukv, b_w_in, b_w_uq):
    win = a_w_in.reshape(D_MODEL, 4, N_CONV_CHUNKS, CONV_CHUNK).transpose(0, 2, 1, 3)
    win = win.reshape(D_MODEL, 4 * E_A).astype(BF16)

    kr = w_dkv[:, KV_RANK:]
    zeros = jnp.zeros((D_MODEL, QK_NOPE), F32)
    wdkv = jnp.concatenate([w_dkv[:, :KV_RANK], zeros, kr, kr,
                            zeros, _swap_halves(kr), _swap_halves(kr)], axis=-1).astype(BF16)

    wk = w_ukv.reshape(KV_RANK, N_HEADS, QK_NOPE + V_HEAD)
    pad_k = jnp.zeros((KV_RANK, N_HEADS, HEAD_PACK - QK_NOPE), F32)
    pad_v = jnp.zeros((KV_RANK, N_HEADS, HEAD_PACK - V_HEAD), F32)
    wkn = jnp.concatenate([wk[..., :QK_NOPE], pad_k], axis=-1).reshape(KV_RANK, N_HEADS * HEAD_PACK)
    wv = jnp.concatenate([wk[..., QK_NOPE:], pad_v], axis=-1).reshape(KV_RANK, N_HEADS * HEAD_PACK)
    wukv = jnp.concatenate([wkn, wv], axis=-1).astype(BF16)

    wcq = b_w_in[:, :Q_RANK].astype(BF16)
    wg = b_w_in[:, Q_RANK:].astype(BF16)

    wq = b_w_uq.reshape(Q_RANK, N_HEADS, QK_NOPE + QK_ROPE)
    rope = wq[..., QK_NOPE:]
    wq = jnp.concatenate([wq[..., :QK_NOPE], rope, _swap_halves(rope)], axis=-1)
    wq = wq.reshape(Q_RANK, N_HEADS * HEAD_PACK).astype(BF16)
    return win, wdkv, wukv, wcq, wg, wq


def kernel(x, positions, a_norm, a_w_in, a_conv, a_w_out, kv_norm, w_dkv, ckv_norm, w_ukv,
           b_norm, b_w_in, b_q_norm, b_w_uq, b_w_out, final_norm):
    bsz, seq, d = x.shape
    assert d == D_MODEL and seq % TOKEN_TILE == 0 and seq % ATT_BLOCK == 0
    assert a_norm.shape[0] == 1 and b_norm.shape[0] == 1, "depth-2 trunk: one conv layer, one MLA layer"

    cc, ss = _rope_lane_tables(positions)
    win, wdkv, wukv, wcq, wg, wq = _prepare_weights(a_w_in[0], w_dkv, w_ukv, b_w_in[0], b_w_uq[0])
    wouta = a_w_out[0].astype(BF16)
    woutb = b_w_out[0].astype(BF16)
    row = lambda g: g.reshape(1, -1).astype(F32)

    tm = TOKEN_TILE
    n_tiles = seq // tm
    hp = N_HEADS * HEAD_PACK
    tok_spec = lambda width: pl.BlockSpec((None, tm, width), lambda b, s: (b, s, 0))

    x1, q, k, v = pl.pallas_call(
        _token_kernel,
        grid=(bsz, n_tiles),
        in_specs=[
            tok_spec(D_MODEL), tok_spec(LANES), tok_spec(LANES),
            _const_spec((1, D_MODEL)), _const_spec(win.shape), _const_spec(a_conv.shape[1:]),
            _const_spec(wouta.shape), _const_spec((1, D_MODEL)), _const_spec(wdkv.shape),
            _const_spec((1, KV_RANK)), _const_spec(wukv.shape), _const_spec((1, D_MODEL)),
            _const_spec(wcq.shape), _const_spec((1, Q_RANK)), _const_spec(wq.shape),
        ],
        out_specs=[tok_spec(D_MODEL), tok_spec(hp), tok_spec(hp), tok_spec(hp)],
        out_shape=[
            jax.ShapeDtypeStruct((bsz, seq, D_MODEL), F32),
            jax.ShapeDtypeStruct((bsz, seq, hp), BF16),
            jax.ShapeDtypeStruct((bsz, seq, hp), BF16),
            jax.ShapeDtypeStruct((bsz, seq, hp), BF16),
        ],
        scratch_shapes=[
            pltpu.VMEM((tm + CARRY_ROWS, E_A), F32),
            pltpu.VMEM((tm, E_A), BF16),
        ],
        compiler_params=pltpu.CompilerParams(
            dimension_semantics=("arbitrary", "arbitrary"),
            vmem_limit_bytes=VMEM_LIMIT_BYTES),
        name="token_kernel",
    )(x, cc, ss, row(a_norm[0]), win, a_conv[0], wouta, row(kv_norm), wdkv, row(ckv_norm),
      wukv, row(b_norm[0]), wcq, row(b_q_norm[0]), wq)

    hw = HEADS_PER_STEP * HEAD_PACK
    att_spec = pl.BlockSpec((None, seq, hw), lambda b, h: (b, 0, h))
    o = pl.pallas_call(
        _attention_kernel,
        grid=(bsz, N_HEADS // HEADS_PER_STEP),
        in_specs=[att_spec, att_spec, att_spec],
        out_specs=pl.BlockSpec((None, seq, HEADS_PER_STEP * V_HEAD), lambda b, h: (b, 0, h)),
        out_shape=jax.ShapeDtypeStruct((bsz, seq, E_B), BF16),
        scratch_shapes=[
            pltpu.VMEM((HEADS_PER_STEP, ATT_BLOCK, 1), F32),
            pltpu.VMEM((HEADS_PER_STEP, ATT_BLOCK, HEAD_PACK), F32),
        ],
        compiler_params=pltpu.CompilerParams(
            dimension_semantics=("arbitrary", "arbitrary"),
            vmem_limit_bytes=VMEM_LIMIT_BYTES),
        name="attention_kernel",
    )(q, k, v)

    out = pl.pallas_call(
        _output_kernel,
        grid=(bsz, n_tiles),
        in_specs=[
            tok_spec(D_MODEL), tok_spec(E_B), _const_spec((1, D_MODEL)), _const_spec(wg.shape),
            _const_spec(woutb.shape), _const_spec((1, D_MODEL)),
        ],
        out_specs=tok_spec(D_MODEL),
        out_shape=jax.ShapeDtypeStruct((bsz, seq, D_MODEL), F32),
        compiler_params=pltpu.CompilerParams(
            dimension_semantics=("arbitrary", "arbitrary"),
            vmem_limit_bytes=VMEM_LIMIT_BYTES),
        name="output_kernel",
    )(x1, o, row(b_norm[0]), wg, woutb, row(final_norm))
    return out
```

```python
import math
from functools import partial

import jax
import jax.numpy as jnp
from jax import lax
from jax.experimental import pallas as pl
from jax.experimental.pallas import tpu as pltpu

D_MODEL = 1024
E_A = D_MODEL
CONV_WIDTH = 3
N_HEADS = 8
QK_NOPE = 64
QK_ROPE = 32
V_HEAD = 64
KV_RANK = 256
Q_RANK = 384
E_B = N_HEADS * V_HEAD
ROPE_THETA = 10000.0
SOFTMAX_SCALE = 1.0 / math.sqrt(QK_NOPE + QK_ROPE)
EPS = 1e-6

LANES = 128
HEAD_PACK = LANES
TOKEN_TILE = 512
CONV_CHUNK = 256
N_CONV_CHUNKS = E_A // CONV_CHUNK
CARRY_ROWS = 8
ATT_BLOCK = 512
ATT_QCHUNK = 256
HEADS_PER_STEP = 2
MASK_VALUE = -1e30
VMEM_LIMIT_BYTES = 56 * 1024 * 1024

BF16 = jnp.bfloat16
F32 = jnp.float32


def _rms_scale(x):
    return lax.rsqrt(jnp.mean(x * x, axis=-1, keepdims=True) + EPS)


def _dot(a, b):
    return jnp.dot(a, b, preferred_element_type=F32)


_NT_DIMS = (((1,), (1,)), ((), ()))


def _token_kernel(x_ref, cc_ref, ss_ref, anorm_ref, win_ref, conv_ref, wouta_ref,
                  kvnorm_ref, wdkv_ref, ckvnorm_ref, wkn_ref, wvt_ref, bnorm_ref, wcq_ref,
                  qnorm_ref, wq_ref,
                  x1_ref, q_ref, k_ref, vt_ref,
                  vbuf_ref, y_ref):
    tm = TOKEN_TILE
    cw = CONV_CHUNK

    @pl.when(pl.program_id(1) == 0)
    def _():
        vbuf_ref[0:CARRY_ROWS, :] = jnp.zeros((CARRY_ROWS, E_A), F32)

    x = x_ref[...]
    h = (x * _rms_scale(x) * anorm_ref[...]).astype(BF16)

    for j in range(N_CONV_CHUNKS):
        cols = slice(j * cw, (j + 1) * cw)
        pj = _dot(h, win_ref[:, j * 4 * cw:(j + 1) * 4 * cw])
        b = pj[:, 0:cw]
        c = pj[:, cw:2 * cw]
        u = pj[:, 2 * cw:3 * cw]
        g = pj[:, 3 * cw:4 * cw]
        v = c * u
        vbuf_ref[CARRY_ROWS:CARRY_ROWS + tm, cols] = v
        vm1 = vbuf_ref[CARRY_ROWS - 1:CARRY_ROWS - 1 + tm, cols]
        vm2 = vbuf_ref[CARRY_ROWS - 2:CARRY_ROWS - 2 + tm, cols]
        conv = conv_ref[0:1, cols] * vm2 + conv_ref[1:2, cols] * vm1 + conv_ref[2:3, cols] * v
        y = (g * jax.nn.sigmoid(g)) * b * conv
        y_ref[:, cols] = y.astype(BF16)

    vbuf_ref[0:CARRY_ROWS, :] = vbuf_ref[tm:tm + CARRY_ROWS, :]

    x1 = x + _dot(y_ref[...], wouta_ref[...])
    x1_ref[...] = x1

    xn = x1 * _rms_scale(x1)
    hkv = (xn * kvnorm_ref[...]).astype(BF16)
    hb = (xn * bnorm_ref[...]).astype(BF16)

    cc = cc_ref[...]
    ss = ss_ref[...]

    ckr = _dot(hkv, wdkv_ref[...])
    ckv = ckr[:, 0:KV_RANK]
    rope_k = ckr[:, KV_RANK:KV_RANK + LANES] * cc + ckr[:, KV_RANK + LANES:KV_RANK + 2 * LANES] * ss
    ckvn = (ckv * _rms_scale(ckv) * ckvnorm_ref[...]).astype(BF16)
    kn = _dot(ckvn, wkn_ref[...])
    vt = lax.dot_general(wvt_ref[...], ckvn, _NT_DIMS, preferred_element_type=F32)
    head_row = lax.broadcasted_iota(jnp.int32, (N_HEADS * HEAD_PACK, 1), 0) % HEAD_PACK
    vt_ref[...] = (vt + jnp.where(head_row == V_HEAD, 1.0, 0.0)).astype(BF16)

    lane = lax.broadcasted_iota(jnp.int32, (1, LANES), 1)

    cq = _dot(hb, wcq_ref[...])
    cqn = (cq * _rms_scale(cq) * qnorm_ref[...]).astype(BF16)
    q = _dot(cqn, wq_ref[...])
    qscale = SOFTMAX_SCALE * math.log2(math.e)
    tq = jnp.where(lane < QK_NOPE, 1.0, jnp.where(lane < QK_NOPE + QK_ROPE, cc, ss)) * qscale

    for hd in range(N_HEADS):
        hc = slice(hd * HEAD_PACK, (hd + 1) * HEAD_PACK)
        k_ref[:, hc] = (kn[:, hc] + rope_k).astype(BF16)
        q_ref[:, hc] = (q[:, hc] * tq).astype(BF16)


def _attention_kernel(q_ref, k_ref, vt_ref, o_ref, m_ref, acc_ref):
    blk = ATT_BLOCK
    nblk = q_ref.shape[0] // blk

    cw = ATT_QCHUNK
    chains = [(hd, c) for hd in range(HEADS_PER_STEP) for c in range(blk // cw)]

    def update(qi, ki, masked):
        krows = pl.ds(pl.multiple_of(ki * blk, blk), blk)
        m_olds = [m_ref[hd, :, c * cw:(c + 1) * cw] for hd, c in chains]
        acc_olds = [acc_ref[hd, :, c * cw:(c + 1) * cw] for hd, c in chains]
        st, m_news, acc_news = {}, {}, {}

        def scores(i):
            hd, c = chains[i]
            hc = slice(hd * HEAD_PACK, (hd + 1) * HEAD_PACK)
            q = q_ref[pl.ds(pl.multiple_of(qi * blk + c * cw, cw), cw), hc]
            k = k_ref[krows, hc]
            s = lax.dot_general(k, q, _NT_DIMS, preferred_element_type=F32)
            if masked:
                kv_idx = lax.broadcasted_iota(jnp.int32, (blk, cw), 0)
                q_idx = lax.broadcasted_iota(jnp.int32, (blk, cw), 1) + c * cw
                s = jnp.where(kv_idx <= q_idx, s, MASK_VALUE)
            st[i] = s
            m_news[i] = jnp.maximum(m_olds[i], jnp.max(s, axis=0, keepdims=True))

        def accumulate(i):
            hd, _ = chains[i]
            hc = slice(hd * HEAD_PACK, (hd + 1) * HEAD_PACK)
            vt = vt_ref[ki, hc, :]
            p = jnp.exp2(st[i] - m_news[i]).astype(BF16)
            alpha = jnp.exp2(m_olds[i] - m_news[i])
            acc_news[i] = alpha * acc_olds[i] + _dot(vt, p)

        scores(0)
        for i in range(len(chains)):
            if i + 1 < len(chains):
                scores(i + 1)
            accumulate(i)
        for i, (hd, c) in enumerate(chains):
            acc_ref[hd, :, c * cw:(c + 1) * cw] = acc_news[i]
            m_ref[hd, :, c * cw:(c + 1) * cw] = m_news[i]

    def q_block(qi, carry):
        m_ref[...] = jnp.full(m_ref.shape, MASK_VALUE, F32)
        acc_ref[...] = jnp.zeros(acc_ref.shape, F32)

        def kv_block(ki, c):
            update(qi, ki, False)
            return c

        lax.fori_loop(0, qi, kv_block, 0)
        update(qi, qi, True)

        qrows = pl.ds(pl.multiple_of(qi * blk, blk), blk)
        for hd in range(HEADS_PER_STEP):
            acc = acc_ref[hd]
            o_t = acc / acc[V_HEAD:V_HEAD + 1, :]
            o_ref[qrows, hd * V_HEAD:(hd + 1) * V_HEAD] = o_t.T[:, 0:V_HEAD].astype(o_ref.dtype)
        return carry

    lax.fori_loop(0, nblk, q_block, 0)


def _output_kernel(x1_ref, o_ref, bnorm_ref, wg_ref, woutb_ref, fnorm_ref, out_ref):
    x1 = x1_ref[...]
    hb = (x1 * _rms_scale(x1) * bnorm_ref[...]).astype(BF16)
    g = _dot(hb, wg_ref[...])
    y = (o_ref[...].astype(F32) * (g * jax.nn.sigmoid(g))).astype(BF16)
    x2 = x1 + _dot(y, woutb_ref[...])
    out_ref[...] = x2 * _rms_scale(x2) * fnorm_ref[...]


def _const_spec(shape):
    return pl.BlockSpec(shape, lambda *_: (0,) * len(shape))


def _rope_lane_tables(positions):
    inv_freq = ROPE_THETA ** (-jnp.arange(0, QK_ROPE, 2, dtype=F32) / QK_ROPE)
    ang = positions.astype(F32)[..., None] * inv_freq
    cos, sin = jnp.cos(ang), jnp.sin(ang)
    reps = LANES // QK_ROPE
    cc = jnp.tile(jnp.concatenate([cos, cos], axis=-1), (1, 1, reps))
    ss = jnp.tile(jnp.concatenate([-sin, sin], axis=-1), (1, 1, reps))
    return cc, ss


def _swap_halves(w):
    half = w.shape[-1] // 2
    return jnp.concatenate([w[..., half:], w[..., :half]], axis=-1)


def _prepare_weights(a_w_in, w_dkv, w_ukv, b_w_in, b_w_uq):
    win = a_w_in.reshape(D_MODEL, 4, N_CONV_CHUNKS, CONV_CHUNK).transpose(0, 2, 1, 3)
    win = win.reshape(D_MODEL, 4 * E_A).astype(BF16)

    kr = w_dkv[:, KV_RANK:]
    zeros = jnp.zeros((D_MODEL, QK_NOPE), F32)
    wdkv = jnp.concatenate([w_dkv[:, :KV_RANK], zeros, kr, kr,
                            zeros, _swap_halves(kr), _swap_halves(kr)], axis=-1).astype(BF16)

    wk = w_ukv.reshape(KV_RANK, N_HEADS, QK_NOPE + V_HEAD)
    pad_k = jnp.zeros((KV_RANK, N_HEADS, HEAD_PACK - QK_NOPE), F32)
    pad_v = jnp.zeros((KV_RANK, N_HEADS, HEAD_PACK - V_HEAD), F32)
    wkn = jnp.concatenate([wk[..., :QK_NOPE], pad_k], axis=-1).reshape(KV_RANK, N_HEADS * HEAD_PACK)
    wv = jnp.concatenate([wk[..., QK_NOPE:], pad_v], axis=-1).reshape(KV_RANK, N_HEADS * HEAD_PACK)
    wkn = wkn.astype(BF16)
    wvt = wv.T.astype(BF16)

    wcq = b_w_in[:, :Q_RANK].astype(BF16)
    wg = b_w_in[:, Q_RANK:].astype(BF16)

    wq = b_w_uq.reshape(Q_RANK, N_HEADS, QK_NOPE + QK_ROPE)
    rope = wq[..., QK_NOPE:]
    wq = jnp.concatenate([wq[..., :QK_NOPE], rope, _swap_halves(rope)], axis=-1)
    wq = wq.reshape(Q_RANK, N_HEADS * HEAD_PACK).astype(BF16)
    return win, wdkv, wkn, wvt, wcq, wg, wq


def kernel(x, positions, a_norm, a_w_in, a_conv, a_w_out, kv_norm, w_dkv, ckv_norm, w_ukv,
           b_norm, b_w_in, b_q_norm, b_w_uq, b_w_out, final_norm):
    bsz, seq, d = x.shape
    assert d == D_MODEL and seq % TOKEN_TILE == 0 and seq % ATT_BLOCK == 0
    assert a_norm.shape[0] == 1 and b_norm.shape[0] == 1, "depth-2 trunk: one conv layer, one MLA layer"

    cc, ss = _rope_lane_tables(positions)
    win, wdkv, wkn, wvt, wcq, wg, wq = _prepare_weights(a_w_in[0], w_dkv, w_ukv, b_w_in[0], b_w_uq[0])
    wouta = a_w_out[0].astype(BF16)
    woutb = b_w_out[0].astype(BF16)
    row = lambda g: g.reshape(1, -1).astype(F32)

    tm = TOKEN_TILE
    assert tm == ATT_BLOCK, "v'^T is written one attention kv block per token tile"
    n_tiles = seq // tm
    hp = N_HEADS * HEAD_PACK
    tok_spec = lambda width: pl.BlockSpec((None, tm, width), lambda b, s: (b, s, 0))

    x1, q, k, vt = pl.pallas_call(
        _token_kernel,
        grid=(bsz, n_tiles),
        in_specs=[
            tok_spec(D_MODEL), tok_spec(LANES), tok_spec(LANES),
            _const_spec((1, D_MODEL)), _const_spec(win.shape), _const_spec(a_conv.shape[1:]),
            _const_spec(wouta.shape), _const_spec((1, D_MODEL)), _const_spec(wdkv.shape),
            _const_spec((1, KV_RANK)), _const_spec(wkn.shape), _const_spec(wvt.shape),
            _const_spec((1, D_MODEL)),
            _const_spec(wcq.shape), _const_spec((1, Q_RANK)), _const_spec(wq.shape),
        ],
        out_specs=[tok_spec(D_MODEL), tok_spec(hp), tok_spec(hp),
                   pl.BlockSpec((None, None, hp, tm), lambda b, s: (b, s, 0, 0))],
        out_shape=[
            jax.ShapeDtypeStruct((bsz, seq, D_MODEL), F32),
            jax.ShapeDtypeStruct((bsz, seq, hp), BF16),
            jax.ShapeDtypeStruct((bsz, seq, hp), BF16),
            jax.ShapeDtypeStruct((bsz, n_tiles, hp, tm), BF16),
        ],
        scratch_shapes=[
            pltpu.VMEM((tm + CARRY_ROWS, E_A), F32),
            pltpu.VMEM((tm, E_A), BF16),
        ],
        compiler_params=pltpu.CompilerParams(
            dimension_semantics=("arbitrary", "arbitrary"),
            vmem_limit_bytes=VMEM_LIMIT_BYTES),
        name="token_kernel",
    )(x, cc, ss, row(a_norm[0]), win, a_conv[0], wouta, row(kv_norm), wdkv, row(ckv_norm),
      wkn, wvt, row(b_norm[0]), wcq, row(b_q_norm[0]), wq)

    hw = HEADS_PER_STEP * HEAD_PACK
    att_spec = pl.BlockSpec((None, seq, hw), lambda b, h: (b, 0, h))
    o = pl.pallas_call(
        _attention_kernel,
        grid=(bsz, N_HEADS // HEADS_PER_STEP),
        in_specs=[att_spec, att_spec,
                  pl.BlockSpec((None, n_tiles, hw, ATT_BLOCK), lambda b, h: (b, 0, h, 0))],
        out_specs=pl.BlockSpec((None, seq, HEADS_PER_STEP * V_HEAD), lambda b, h: (b, 0, h)),
        out_shape=jax.ShapeDtypeStruct((bsz, seq, E_B), BF16),
        scratch_shapes=[
            pltpu.VMEM((HEADS_PER_STEP, 1, ATT_BLOCK), F32),
            pltpu.VMEM((HEADS_PER_STEP, HEAD_PACK, ATT_BLOCK), F32),
        ],
        compiler_params=pltpu.CompilerParams(
            dimension_semantics=("arbitrary", "arbitrary"),
            vmem_limit_bytes=VMEM_LIMIT_BYTES),
        name="attention_kernel",
    )(q, k, vt)

    out = pl.pallas_call(
        _output_kernel,
        grid=(bsz, n_tiles),
        in_specs=[
            tok_spec(D_MODEL), tok_spec(E_B), _const_spec((1, D_MODEL)), _const_spec(wg.shape),
            _const_spec(woutb.shape), _const_spec((1, D_MODEL)),
        ],
        out_specs=tok_spec(D_MODEL),
        out_shape=jax.ShapeDtypeStruct((bsz, seq, D_MODEL), F32),
        compiler_params=pltpu.CompilerParams(
            dimension_semantics=("arbitrary", "arbitrary"),
            vmem_limit_bytes=VMEM_LIMIT_BYTES),
        name="output_kernel",
    )(x1, o, row(b_norm[0]), wg, woutb, row(final_norm))
    return out
```

```python
import math
from functools import partial

import jax
import jax.numpy as jnp
from jax import lax
from jax.experimental import pallas as pl
from jax.experimental.pallas import tpu as pltpu

D_MODEL = 1024
E_A = D_MODEL
CONV_WIDTH = 3
N_HEADS = 8
QK_NOPE = 64
QK_ROPE = 32
V_HEAD = 64
KV_RANK = 256
Q_RANK = 384
E_B = N_HEADS * V_HEAD
ROPE_THETA = 10000.0
SOFTMAX_SCALE = 1.0 / math.sqrt(QK_NOPE + QK_ROPE)
EPS = 1e-6

LANES = 128
HEAD_PACK = LANES
TOKEN_TILE = 512
CONV_CHUNK = 256
N_CONV_CHUNKS = E_A // CONV_CHUNK
CARRY_ROWS = 8
ATT_BLOCK = 512
ATT_QCHUNK = 256
HEADS_PER_STEP = 2
MASK_VALUE = -1e30
VMEM_LIMIT_BYTES = 56 * 1024 * 1024

BF16 = jnp.bfloat16
F32 = jnp.float32


def _rms_scale(x):
    return lax.rsqrt(jnp.mean(x * x, axis=-1, keepdims=True) + EPS)


def _dot(a, b):
    return jnp.dot(a, b, preferred_element_type=F32)


_NT_DIMS = (((1,), (1,)), ((), ()))


def _token_kernel(x_ref, cc_ref, ss_ref, anorm_ref, win_ref, conv_ref, wouta_ref,
                  kvnorm_ref, wdkv_ref, ckvnorm_ref, wkn_ref, wvt_ref, bnorm_ref, wcq_ref,
                  qnorm_ref, wq_ref,
                  x1_ref, q_ref, k_ref, vt_ref,
                  vbuf_ref, y_ref):
    tm = TOKEN_TILE
    cw = CONV_CHUNK

    @pl.when(pl.program_id(1) == 0)
    def _():
        vbuf_ref[0:CARRY_ROWS, :] = jnp.zeros((CARRY_ROWS, E_A), F32)

    x = x_ref[...]
    h = (x * _rms_scale(x) * anorm_ref[...]).astype(BF16)

    for j in range(N_CONV_CHUNKS):
        cols = slice(j * cw, (j + 1) * cw)
        pj = _dot(h, win_ref[:, j * 4 * cw:(j + 1) * 4 * cw])
        b = pj[:, 0:cw]
        c = pj[:, cw:2 * cw]
        u = pj[:, 2 * cw:3 * cw]
        g = pj[:, 3 * cw:4 * cw]
        v = c * u
        vbuf_ref[CARRY_ROWS:CARRY_ROWS + tm, cols] = v
        vm1 = vbuf_ref[CARRY_ROWS - 1:CARRY_ROWS - 1 + tm, cols]
        vm2 = vbuf_ref[CARRY_ROWS - 2:CARRY_ROWS - 2 + tm, cols]
        conv = conv_ref[0:1, cols] * vm2 + conv_ref[1:2, cols] * vm1 + conv_ref[2:3, cols] * v
        y = (g * jax.nn.sigmoid(g)) * b * conv
        y_ref[:, cols] = y.astype(BF16)

    vbuf_ref[0:CARRY_ROWS, :] = vbuf_ref[tm:tm + CARRY_ROWS, :]

    x1 = x + _dot(y_ref[...], wouta_ref[...])
    x1_ref[...] = x1

    xn = x1 * _rms_scale(x1)
    hkv = (xn * kvnorm_ref[...]).astype(BF16)
    hb = (xn * bnorm_ref[...]).astype(BF16)

    cc = cc_ref[...]
    ss = ss_ref[...]

    ckr = _dot(hkv, wdkv_ref[...])
    ckv = ckr[:, 0:KV_RANK]
    rope_k = ckr[:, KV_RANK:KV_RANK + LANES] * cc + ckr[:, KV_RANK + LANES:KV_RANK + 2 * LANES] * ss
    ckvn = (ckv * _rms_scale(ckv) * ckvnorm_ref[...]).astype(BF16)
    kn = _dot(ckvn, wkn_ref[...])
    vt = lax.dot_general(wvt_ref[...], ckvn, _NT_DIMS, preferred_element_type=F32)
    head_row = lax.broadcasted_iota(jnp.int32, (N_HEADS * HEAD_PACK, 1), 0) % HEAD_PACK
    vt_ref[...] = (vt + jnp.where(head_row == V_HEAD, 1.0, 0.0)).astype(BF16)

    lane = lax.broadcasted_iota(jnp.int32, (1, LANES), 1)

    cq = _dot(hb, wcq_ref[...])
    cqn = (cq * _rms_scale(cq) * qnorm_ref[...]).astype(BF16)
    q = _dot(cqn, wq_ref[...])
    qscale = SOFTMAX_SCALE * math.log2(math.e)
    tq = jnp.where(lane < QK_NOPE, 1.0, jnp.where(lane < QK_NOPE + QK_ROPE, cc, ss)) * qscale

    for hd in range(N_HEADS):
        hc = slice(hd * HEAD_PACK, (hd + 1) * HEAD_PACK)
        k_ref[:, hc] = (kn[:, hc] + rope_k).astype(BF16)
        q_ref[:, hc] = (q[:, hc] * tq).astype(BF16)


def _attention_kernel(q_ref, k_ref, vt_ref, o_ref, m_ref, alpha_ref, acc_ref, *st_refs):
    blk = ATT_BLOCK
    cw = ATT_QCHUNK
    nblk = q_ref.shape[0] // blk
    chains = [(hd, c) for hd in range(HEADS_PER_STEP) for c in range(blk // cw)]
    assert len(st_refs) == len(chains)

    def head_cols(i):
        hd = chains[i][0]
        return slice(hd * HEAD_PACK, (hd + 1) * HEAD_PACK)

    def state(ref, i):
        hd, c = chains[i]
        return ref.at[hd, :, c * cw:(c + 1) * cw]

    def qk(i, qi, ki):
        c = chains[i][1]
        q = q_ref[pl.ds(pl.multiple_of(qi * blk + c * cw, cw), cw), head_cols(i)]
        k = k_ref[pl.ds(pl.multiple_of(ki * blk, blk), blk), head_cols(i)]
        return lax.dot_general(k, q, _NT_DIMS, preferred_element_type=F32)

    def stage_scores(i, s, diagonal):
        if diagonal:
            c = chains[i][1]
            kv_idx = lax.broadcasted_iota(jnp.int32, (blk, cw), 0)
            q_idx = lax.broadcasted_iota(jnp.int32, (blk, cw), 1) + c * cw
            s = jnp.where(kv_idx <= q_idx, s, MASK_VALUE)
            m_old = jnp.full((1, cw), MASK_VALUE, F32)
        else:
            m_old = state(m_ref, i)[...]
        m_new = jnp.maximum(m_old, jnp.max(s, axis=0, keepdims=True))
        st_refs[i][...] = s
        state(alpha_ref, i)[...] = jnp.exp2(m_old - m_new)
        state(m_ref, i)[...] = m_new

    def accumulate(i, ki):
        vt = vt_ref[ki, head_cols(i), :]
        p = jnp.exp2(st_refs[i][...] - state(m_ref, i)[...]).astype(BF16)
        acc = state(acc_ref, i)
        acc[...] = state(alpha_ref, i)[...] * acc[...] + _dot(vt, p)

    def step(qi, ki, diagonal, staged_ki):
        s = {0: qk(0, qi, ki)}
        for i in range(len(chains)):
            if i + 1 < len(chains):
                s[i + 1] = qk(i + 1, qi, ki)
            accumulate(i, staged_ki)
            stage_scores(i, s.pop(i), diagonal)

    def finalize(qi):
        qrows = pl.ds(pl.multiple_of(qi * blk, blk), blk)
        for hd in range(HEADS_PER_STEP):
            acc = acc_ref[hd]
            o_t = acc / acc[V_HEAD:V_HEAD + 1, :]
            o_ref[qrows, hd * V_HEAD:(hd + 1) * V_HEAD] = o_t.T[:, 0:V_HEAD].astype(o_ref.dtype)

    acc_ref[...] = jnp.zeros(acc_ref.shape, F32)
    for i in range(len(chains)):
        stage_scores(i, qk(i, 0, 0), True)

    def q_block(qi, staged_ki):
        step(qi, qi, True, staged_ki)
        finalize(qi - 1)

        def kv_block(ki, staged):
            step(qi, ki, False, staged)
            return ki

        return lax.fori_loop(0, qi, kv_block, qi)

    staged_ki = lax.fori_loop(1, nblk, q_block, 0)
    for i in range(len(chains)):
        accumulate(i, staged_ki)
    finalize(nblk - 1)


def _output_kernel(x1_ref, o_ref, bnorm_ref, wg_ref, woutb_ref, fnorm_ref, out_ref):
    x1 = x1_ref[...]
    hb = (x1 * _rms_scale(x1) * bnorm_ref[...]).astype(BF16)
    g = _dot(hb, wg_ref[...])
    y = (o_ref[...].astype(F32) * (g * jax.nn.sigmoid(g))).astype(BF16)
    x2 = x1 + _dot(y, woutb_ref[...])
    out_ref[...] = x2 * _rms_scale(x2) * fnorm_ref[...]


def _const_spec(shape):
    return pl.BlockSpec(shape, lambda *_: (0,) * len(shape))


def _rope_lane_tables(positions):
    inv_freq = ROPE_THETA ** (-jnp.arange(0, QK_ROPE, 2, dtype=F32) / QK_ROPE)
    ang = positions.astype(F32)[..., None] * inv_freq
    cos, sin = jnp.cos(ang), jnp.sin(ang)
    reps = LANES // QK_ROPE
    cc = jnp.tile(jnp.concatenate([cos, cos], axis=-1), (1, 1, reps))
    ss = jnp.tile(jnp.concatenate([-sin, sin], axis=-1), (1, 1, reps))
    return cc, ss


def _swap_halves(w):
    half = w.shape[-1] // 2
    return jnp.concatenate([w[..., half:], w[..., :half]], axis=-1)


def _prepare_weights(a_w_in, w_dkv, w_ukv, b_w_in, b_w_uq):
    win = a_w_in.reshape(D_MODEL, 4, N_CONV_CHUNKS, CONV_CHUNK).transpose(0, 2, 1, 3)
    win = win.reshape(D_MODEL, 4 * E_A).astype(BF16)

    kr = w_dkv[:, KV_RANK:]
    zeros = jnp.zeros((D_MODEL, QK_NOPE), F32)
    wdkv = jnp.concatenate([w_dkv[:, :KV_RANK], zeros, kr, kr,
                            zeros, _swap_halves(kr), _swap_halves(kr)], axis=-1).astype(BF16)

    wk = w_ukv.reshape(KV_RANK, N_HEADS, QK_NOPE + V_HEAD)
    pad_k = jnp.zeros((KV_RANK, N_HEADS, HEAD_PACK - QK_NOPE), F32)
    pad_v = jnp.zeros((KV_RANK, N_HEADS, HEAD_PACK - V_HEAD), F32)
    wkn = jnp.concatenate([wk[..., :QK_NOPE], pad_k], axis=-1).reshape(KV_RANK, N_HEADS * HEAD_PACK)
    wv = jnp.concatenate([wk[..., QK_NOPE:], pad_v], axis=-1).reshape(KV_RANK, N_HEADS * HEAD_PACK)
    wkn = wkn.astype(BF16)
    wvt = wv.T.astype(BF16)

    wcq = b_w_in[:, :Q_RANK].astype(BF16)
    wg = b_w_in[:, Q_RANK:].astype(BF16)

    wq = b_w_uq.reshape(Q_RANK, N_HEADS, QK_NOPE + QK_ROPE)
    rope = wq[..., QK_NOPE:]
    wq = jnp.concatenate([wq[..., :QK_NOPE], rope, _swap_halves(rope)], axis=-1)
    wq = wq.reshape(Q_RANK, N_HEADS * HEAD_PACK).astype(BF16)
    return win, wdkv, wkn, wvt, wcq, wg, wq


def kernel(x, positions, a_norm, a_w_in, a_conv, a_w_out, kv_norm, w_dkv, ckv_norm, w_ukv,
           b_norm, b_w_in, b_q_norm, b_w_uq, b_w_out, final_norm):
    bsz, seq, d = x.shape
    assert d == D_MODEL and seq % TOKEN_TILE == 0 and seq % ATT_BLOCK == 0
    assert a_norm.shape[0] == 1 and b_norm.shape[0] == 1, "depth-2 trunk: one conv layer, one MLA layer"

    cc, ss = _rope_lane_tables(positions)
    win, wdkv, wkn, wvt, wcq, wg, wq = _prepare_weights(a_w_in[0], w_dkv, w_ukv, b_w_in[0], b_w_uq[0])
    wouta = a_w_out[0].astype(BF16)
    woutb = b_w_out[0].astype(BF16)
    row = lambda g: g.reshape(1, -1).astype(F32)

    tm = TOKEN_TILE
    assert tm == ATT_BLOCK, "v'^T is written one attention kv block per token tile"
    n_tiles = seq // tm
    hp = N_HEADS * HEAD_PACK
    tok_spec = lambda width: pl.BlockSpec((None, tm, width), lambda b, s: (b, s, 0))

    x1, q, k, vt = pl.pallas_call(
        _token_kernel,
        grid=(bsz, n_tiles),
        in_specs=[
            tok_spec(D_MODEL), tok_spec(LANES), tok_spec(LANES),
            _const_spec((1, D_MODEL)), _const_spec(win.shape), _const_spec(a_conv.shape[1:]),
            _const_spec(wouta.shape), _const_spec((1, D_MODEL)), _const_spec(wdkv.shape),
            _const_spec((1, KV_RANK)), _const_spec(wkn.shape), _const_spec(wvt.shape),
            _const_spec((1, D_MODEL)),
            _const_spec(wcq.shape), _const_spec((1, Q_RANK)), _const_spec(wq.shape),
        ],
        out_specs=[tok_spec(D_MODEL), tok_spec(hp), tok_spec(hp),
                   pl.BlockSpec((None, None, hp, tm), lambda b, s: (b, s, 0, 0))],
        out_shape=[
            jax.ShapeDtypeStruct((bsz, seq, D_MODEL), F32),
            jax.ShapeDtypeStruct((bsz, seq, hp), BF16),
            jax.ShapeDtypeStruct((bsz, seq, hp), BF16),
            jax.ShapeDtypeStruct((bsz, n_tiles, hp, tm), BF16),
        ],
        scratch_shapes=[
            pltpu.VMEM((tm + CARRY_ROWS, E_A), F32),
            pltpu.VMEM((tm, E_A), BF16),
        ],
        compiler_params=pltpu.CompilerParams(
            dimension_semantics=("arbitrary", "arbitrary"),
            vmem_limit_bytes=VMEM_LIMIT_BYTES),
        name="token_kernel",
    )(x, cc, ss, row(a_norm[0]), win, a_conv[0], wouta, row(kv_norm), wdkv, row(ckv_norm),
      wkn, wvt, row(b_norm[0]), wcq, row(b_q_norm[0]), wq)

    hw = HEADS_PER_STEP * HEAD_PACK
    att_spec = pl.BlockSpec((None, seq, hw), lambda b, h: (b, 0, h))
    o = pl.pallas_call(
        _attention_kernel,
        grid=(bsz, N_HEADS // HEADS_PER_STEP),
        in_specs=[att_spec, att_spec,
                  pl.BlockSpec((None, n_tiles, hw, ATT_BLOCK), lambda b, h: (b, 0, h, 0))],
        out_specs=pl.BlockSpec((None, seq, HEADS_PER_STEP * V_HEAD), lambda b, h: (b, 0, h)),
        out_shape=jax.ShapeDtypeStruct((bsz, seq, E_B), BF16),
        scratch_shapes=[
            pltpu.VMEM((HEADS_PER_STEP, 1, ATT_BLOCK), F32),
            pltpu.VMEM((HEADS_PER_STEP, 1, ATT_BLOCK), F32),
            pltpu.VMEM((HEADS_PER_STEP, HEAD_PACK, ATT_BLOCK), F32),
        ] + [pltpu.VMEM((ATT_BLOCK, ATT_QCHUNK), F32)
             for _ in range(HEADS_PER_STEP * (ATT_BLOCK // ATT_QCHUNK))],
        compiler_params=pltpu.CompilerParams(
            dimension_semantics=("arbitrary", "arbitrary"),
            vmem_limit_bytes=VMEM_LIMIT_BYTES),
        name="attention_kernel",
    )(q, k, vt)

    out = pl.pallas_call(
        _output_kernel,
        grid=(bsz, n_tiles),
        in_specs=[
            tok_spec(D_MODEL), tok_spec(E_B), _const_spec((1, D_MODEL)), _const_spec(wg.shape),
            _const_spec(woutb.shape), _const_spec((1, D_MODEL)),
        ],
        out_specs=tok_spec(D_MODEL),
        out_shape=jax.ShapeDtypeStruct((bsz, seq, D_MODEL), F32),
        compiler_params=pltpu.CompilerParams(
            dimension_semantics=("arbitrary", "arbitrary"),
            vmem_limit_bytes=VMEM_LIMIT_BYTES),
        name="output_kernel",
    )(x1, o, row(b_norm[0]), wg, woutb, row(final_norm))
    return out
```

```python
import math
from functools import partial

import jax
import jax.numpy as jnp
from jax import lax
from jax.experimental import pallas as pl
from jax.experimental.pallas import tpu as pltpu

D_MODEL = 1024
E_A = D_MODEL
CONV_WIDTH = 3
N_HEADS = 8
QK_NOPE = 64
QK_ROPE = 32
V_HEAD = 64
KV_RANK = 256
Q_RANK = 384
E_B = N_HEADS * V_HEAD
ROPE_THETA = 10000.0
SOFTMAX_SCALE = 1.0 / math.sqrt(QK_NOPE + QK_ROPE)
EPS = 1e-6

LANES = 128
HEAD_PACK = LANES
BF16_SUBLANES = 16
V_PACK = V_HEAD + BF16_SUBLANES
TOKEN_TILE = 512
CONV_CHUNK = 256
N_CONV_CHUNKS = E_A // CONV_CHUNK
CARRY_ROWS = 8
ATT_BLOCK = 512
ATT_QCHUNK = 256
HEADS_PER_STEP = 4
QK_LOOKAHEAD = 2
MASK_VALUE = -1e30
VMEM_LIMIT_BYTES = 56 * 1024 * 1024

BF16 = jnp.bfloat16
F32 = jnp.float32


def _rms_scale(x):
    return lax.rsqrt(jnp.mean(x * x, axis=-1, keepdims=True) + EPS)


def _dot(a, b):
    return jnp.dot(a, b, preferred_element_type=F32)


_NT_DIMS = (((1,), (1,)), ((), ()))


def _token_kernel(x_ref, cc_ref, ss_ref, anorm_ref, win_ref, conv_ref, wouta_ref,
                  kvnorm_ref, wdkv_ref, ckvnorm_ref, wkn_ref, wvt_ref, bnorm_ref, wcq_ref,
                  qnorm_ref, wq_ref,
                  x1_ref, q_ref, k_ref, vt_ref,
                  vbuf_ref, y_ref):
    tm = TOKEN_TILE
    cw = CONV_CHUNK

    @pl.when(pl.program_id(1) == 0)
    def _():
        vbuf_ref[0:CARRY_ROWS, :] = jnp.zeros((CARRY_ROWS, E_A), F32)

    x = x_ref[...]
    h = (x * _rms_scale(x) * anorm_ref[...]).astype(BF16)

    for j in range(N_CONV_CHUNKS):
        cols = slice(j * cw, (j + 1) * cw)
        pj = _dot(h, win_ref[:, j * 4 * cw:(j + 1) * 4 * cw])
        b = pj[:, 0:cw]
        c = pj[:, cw:2 * cw]
        u = pj[:, 2 * cw:3 * cw]
        g = pj[:, 3 * cw:4 * cw]
        v = c * u
        vbuf_ref[CARRY_ROWS:CARRY_ROWS + tm, cols] = v
        vm1 = vbuf_ref[CARRY_ROWS - 1:CARRY_ROWS - 1 + tm, cols]
        vm2 = vbuf_ref[CARRY_ROWS - 2:CARRY_ROWS - 2 + tm, cols]
        conv = conv_ref[0:1, cols] * vm2 + conv_ref[1:2, cols] * vm1 + conv_ref[2:3, cols] * v
        y = (g * jax.nn.sigmoid(g)) * b * conv
        y_ref[:, cols] = y.astype(BF16)

    vbuf_ref[0:CARRY_ROWS, :] = vbuf_ref[tm:tm + CARRY_ROWS, :]

    x1 = x + _dot(y_ref[...], wouta_ref[...])
    x1_ref[...] = x1

    xn = x1 * _rms_scale(x1)
    hkv = (xn * kvnorm_ref[...]).astype(BF16)
    hb = (xn * bnorm_ref[...]).astype(BF16)

    cc = cc_ref[...]
    ss = ss_ref[...]

    ckr = _dot(hkv, wdkv_ref[...])
    ckv = ckr[:, 0:KV_RANK]
    rope_k = ckr[:, KV_RANK:KV_RANK + LANES] * cc + ckr[:, KV_RANK + LANES:KV_RANK + 2 * LANES] * ss
    ckvn = (ckv * _rms_scale(ckv) * ckvnorm_ref[...]).astype(BF16)
    kn = _dot(ckvn, wkn_ref[...])
    vt = lax.dot_general(wvt_ref[...], ckvn, _NT_DIMS, preferred_element_type=F32)
    head_row = lax.broadcasted_iota(jnp.int32, (N_HEADS * V_PACK, 1), 0) % V_PACK
    vt_ref[...] = (vt + jnp.where(head_row == V_HEAD, 1.0, 0.0)).astype(BF16)

    lane = lax.broadcasted_iota(jnp.int32, (1, LANES), 1)

    cq = _dot(hb, wcq_ref[...])
    cqn = (cq * _rms_scale(cq) * qnorm_ref[...]).astype(BF16)
    q = _dot(cqn, wq_ref[...])
    qscale = SOFTMAX_SCALE * math.log2(math.e)
    tq = jnp.where(lane < QK_NOPE, 1.0, jnp.where(lane < QK_NOPE + QK_ROPE, cc, ss)) * qscale

    for hd in range(N_HEADS):
        hc = slice(hd * HEAD_PACK, (hd + 1) * HEAD_PACK)
        k_ref[:, hc] = (kn[:, hc] + rope_k).astype(BF16)
        q_ref[:, hc] = (q[:, hc] * tq).astype(BF16)


def _attention_kernel(q_ref, k_ref, vt_ref, o_ref, m_ref, alpha_ref, acc_ref, *st_refs):
    blk = ATT_BLOCK
    cw = ATT_QCHUNK
    nblk = q_ref.shape[0] // blk
    chains = [(hd, c) for hd in range(HEADS_PER_STEP) for c in range(blk // cw)]
    assert len(st_refs) == len(chains)

    def head_cols(i):
        hd = chains[i][0]
        return slice(hd * HEAD_PACK, (hd + 1) * HEAD_PACK)

    def state(ref, i):
        hd, c = chains[i]
        return ref.at[hd, :, c * cw:(c + 1) * cw]

    def qk(i, qi, ki, diagonal):
        c = chains[i][1]
        rows = (c + 1) * cw if diagonal else blk
        q = q_ref[pl.ds(pl.multiple_of(qi * blk + c * cw, cw), cw), head_cols(i)]
        k = k_ref[pl.ds(pl.multiple_of(ki * blk, blk), rows), head_cols(i)]
        return lax.dot_general(k, q, _NT_DIMS, preferred_element_type=F32)

    def stage_scores(i, s, diagonal):
        st = st_refs[i]
        if diagonal:
            c = chains[i][1]
            lo, hi = c * cw, (c + 1) * cw
            kv_idx = lax.broadcasted_iota(jnp.int32, (cw, cw), 0)
            q_idx = lax.broadcasted_iota(jnp.int32, (cw, cw), 1)
            tri = jnp.where(kv_idx <= q_idx, s[lo:hi], MASK_VALUE)
            m_new = jnp.max(tri, axis=0, keepdims=True)
            st[lo:hi, :] = tri
            if lo > 0:
                m_new = jnp.maximum(m_new, jnp.max(s[0:lo], axis=0, keepdims=True))
                st[0:lo, :] = s[0:lo]
            if hi < blk:
                st[hi:blk, :] = jnp.full((blk - hi, cw), MASK_VALUE, F32)
            m_old = jnp.full((1, cw), MASK_VALUE, F32)
        else:
            m_old = state(m_ref, i)[...]
            m_new = jnp.maximum(m_old, jnp.max(s, axis=0, keepdims=True))
            st[...] = s
        state(alpha_ref, i)[...] = jnp.exp2(m_old - m_new)
        state(m_ref, i)[...] = m_new

    def accumulate(i, ki):
        hd = chains[i][0]
        vt = vt_ref[ki, hd * V_PACK:(hd + 1) * V_PACK, :]
        p = jnp.exp2(st_refs[i][...] - state(m_ref, i)[...]).astype(BF16)
        acc = state(acc_ref, i)
        acc[...] = state(alpha_ref, i)[...] * acc[...] + _dot(vt, p)

    def step(qi, ki, diagonal, staged_ki):
        s = {j: qk(j, qi, ki, diagonal) for j in range(QK_LOOKAHEAD)}
        for i in range(len(chains)):
            if i + QK_LOOKAHEAD < len(chains):
                s[i + QK_LOOKAHEAD] = qk(i + QK_LOOKAHEAD, qi, ki, diagonal)
            accumulate(i, staged_ki)
            stage_scores(i, s.pop(i), diagonal)

    def finalize(qi):
        o_t = jnp.concatenate([acc_ref[hd, 0:V_HEAD, :] / acc_ref[hd, V_HEAD:V_HEAD + 1, :]
                               for hd in range(HEADS_PER_STEP)], axis=0)
        o_ref[pl.ds(pl.multiple_of(qi * blk, blk), blk), :] = o_t.T.astype(o_ref.dtype)

    acc_ref[...] = jnp.zeros(acc_ref.shape, F32)
    for i in range(len(chains)):
        stage_scores(i, qk(i, 0, 0, True), True)

    def q_block(qi, staged_ki):
        step(qi, qi, True, staged_ki)
        finalize(qi - 1)

        def kv_block(ki, staged):
            step(qi, ki, False, staged)
            return ki

        return lax.fori_loop(0, qi, kv_block, qi)

    staged_ki = lax.fori_loop(1, nblk, q_block, 0)
    for i in range(len(chains)):
        accumulate(i, staged_ki)
    finalize(nblk - 1)


def _output_kernel(x1_ref, o_ref, bnorm_ref, wg_ref, woutb_ref, fnorm_ref, out_ref):
    x1 = x1_ref[...]
    hb = (x1 * _rms_scale(x1) * bnorm_ref[...]).astype(BF16)
    g = _dot(hb, wg_ref[...])
    y = (o_ref[...].astype(F32) * (g * jax.nn.sigmoid(g))).astype(BF16)
    x2 = x1 + _dot(y, woutb_ref[...])
    out_ref[...] = x2 * _rms_scale(x2) * fnorm_ref[...]


def _const_spec(shape):
    return pl.BlockSpec(shape, lambda *_: (0,) * len(shape))


def _rope_lane_tables(positions):
    inv_freq = ROPE_THETA ** (-jnp.arange(0, QK_ROPE, 2, dtype=F32) / QK_ROPE)
    ang = positions.astype(F32)[..., None] * inv_freq
    cos, sin = jnp.cos(ang), jnp.sin(ang)
    reps = LANES // QK_ROPE
    cc = jnp.tile(jnp.concatenate([cos, cos], axis=-1), (1, 1, reps))
    ss = jnp.tile(jnp.concatenate([-sin, sin], axis=-1), (1, 1, reps))
    return cc, ss


def _swap_halves(w):
    half = w.shape[-1] // 2
    return jnp.concatenate([w[..., half:], w[..., :half]], axis=-1)


def _prepare_weights(a_w_in, w_dkv, w_ukv, b_w_in, b_w_uq):
    win = a_w_in.reshape(D_MODEL, 4, N_CONV_CHUNKS, CONV_CHUNK).transpose(0, 2, 1, 3)
    win = win.reshape(D_MODEL, 4 * E_A).astype(BF16)

    kr = w_dkv[:, KV_RANK:]
    zeros = jnp.zeros((D_MODEL, QK_NOPE), F32)
    wdkv = jnp.concatenate([w_dkv[:, :KV_RANK], zeros, kr, kr,
                            zeros, _swap_halves(kr), _swap_halves(kr)], axis=-1).astype(BF16)

    wk = w_ukv.reshape(KV_RANK, N_HEADS, QK_NOPE + V_HEAD)
    pad_k = jnp.zeros((KV_RANK, N_HEADS, HEAD_PACK - QK_NOPE), F32)
    wkn = jnp.concatenate([wk[..., :QK_NOPE], pad_k], axis=-1).reshape(KV_RANK, N_HEADS * HEAD_PACK)
    wkn = wkn.astype(BF16)
    pad_v = jnp.zeros((KV_RANK, N_HEADS, V_PACK - V_HEAD), F32)
    wvt = jnp.concatenate([wk[..., QK_NOPE:], pad_v], axis=-1).reshape(KV_RANK, N_HEADS * V_PACK)
    wvt = wvt.T.astype(BF16)

    wcq = b_w_in[:, :Q_RANK].astype(BF16)
    wg = b_w_in[:, Q_RANK:].astype(BF16)

    wq = b_w_uq.reshape(Q_RANK, N_HEADS, QK_NOPE + QK_ROPE)
    rope = wq[..., QK_NOPE:]
    wq = jnp.concatenate([wq[..., :QK_NOPE], rope, _swap_halves(rope)], axis=-1)
    wq = wq.reshape(Q_RANK, N_HEADS * HEAD_PACK).astype(BF16)
    return win, wdkv, wkn, wvt, wcq, wg, wq


def kernel(x, positions, a_norm, a_w_in, a_conv, a_w_out, kv_norm, w_dkv, ckv_norm, w_ukv,
           b_norm, b_w_in, b_q_norm, b_w_uq, b_w_out, final_norm):
    bsz, seq, d = x.shape
    assert d == D_MODEL and seq % TOKEN_TILE == 0 and seq % ATT_BLOCK == 0
    assert a_norm.shape[0] == 1 and b_norm.shape[0] == 1, "depth-2 trunk: one conv layer, one MLA layer"

    cc, ss = _rope_lane_tables(positions)
    win, wdkv, wkn, wvt, wcq, wg, wq = _prepare_weights(a_w_in[0], w_dkv, w_ukv, b_w_in[0], b_w_uq[0])
    wouta = a_w_out[0].astype(BF16)
    woutb = b_w_out[0].astype(BF16)
    row = lambda g: g.reshape(1, -1).astype(F32)

    tm = TOKEN_TILE
    assert tm == ATT_BLOCK, "v'^T is written one attention kv block per token tile"
    n_tiles = seq // tm
    hp = N_HEADS * HEAD_PACK
    tok_spec = lambda width: pl.BlockSpec((None, tm, width), lambda b, s: (b, s, 0))

    x1, q, k, vt = pl.pallas_call(
        _token_kernel,
        grid=(bsz, n_tiles),
        in_specs=[
            tok_spec(D_MODEL), tok_spec(LANES), tok_spec(LANES),
            _const_spec((1, D_MODEL)), _const_spec(win.shape), _const_spec(a_conv.shape[1:]),
            _const_spec(wouta.shape), _const_spec((1, D_MODEL)), _const_spec(wdkv.shape),
            _const_spec((1, KV_RANK)), _const_spec(wkn.shape), _const_spec(wvt.shape),
            _const_spec((1, D_MODEL)),
            _const_spec(wcq.shape), _const_spec((1, Q_RANK)), _const_spec(wq.shape),
        ],
        out_specs=[tok_spec(D_MODEL), tok_spec(hp), tok_spec(hp),
                   pl.BlockSpec((None, None, N_HEADS * V_PACK, tm), lambda b, s: (b, s, 0, 0))],
        out_shape=[
            jax.ShapeDtypeStruct((bsz, seq, D_MODEL), F32),
            jax.ShapeDtypeStruct((bsz, seq, hp), BF16),
            jax.ShapeDtypeStruct((bsz, seq, hp), BF16),
            jax.ShapeDtypeStruct((bsz, n_tiles, N_HEADS * V_PACK, tm), BF16),
        ],
        scratch_shapes=[
            pltpu.VMEM((tm + CARRY_ROWS, E_A), F32),
            pltpu.VMEM((tm, E_A), BF16),
        ],
        compiler_params=pltpu.CompilerParams(
            dimension_semantics=("arbitrary", "arbitrary"),
            vmem_limit_bytes=VMEM_LIMIT_BYTES),
        name="token_kernel",
    )(x, cc, ss, row(a_norm[0]), win, a_conv[0], wouta, row(kv_norm), wdkv, row(ckv_norm),
      wkn, wvt, row(b_norm[0]), wcq, row(b_q_norm[0]), wq)

    hw = HEADS_PER_STEP * HEAD_PACK
    att_spec = pl.BlockSpec((None, seq, hw), lambda b, h: (b, 0, h))
    o = pl.pallas_call(
        _attention_kernel,
        grid=(bsz, N_HEADS // HEADS_PER_STEP),
        in_specs=[att_spec, att_spec,
                  pl.BlockSpec((None, n_tiles, HEADS_PER_STEP * V_PACK, ATT_BLOCK),
                               lambda b, h: (b, 0, h, 0))],
        out_specs=pl.BlockSpec((None, seq, HEADS_PER_STEP * V_HEAD), lambda b, h: (b, 0, h)),
        out_shape=jax.ShapeDtypeStruct((bsz, seq, E_B), BF16),
        scratch_shapes=[
            pltpu.VMEM((HEADS_PER_STEP, 1, ATT_BLOCK), F32),
            pltpu.VMEM((HEADS_PER_STEP, 1, ATT_BLOCK), F32),
            pltpu.VMEM((HEADS_PER_STEP, V_PACK, ATT_BLOCK), F32),
        ] + [pltpu.VMEM((ATT_BLOCK, ATT_QCHUNK), F32)
             for _ in range(HEADS_PER_STEP * (ATT_BLOCK // ATT_QCHUNK))],
        compiler_params=pltpu.CompilerParams(
            dimension_semantics=("arbitrary", "arbitrary"),
            vmem_limit_bytes=VMEM_LIMIT_BYTES),
        name="attention_kernel",
    )(q, k, vt)

    out = pl.pallas_call(
        _output_kernel,
        grid=(bsz, n_tiles),
        in_specs=[
            tok_spec(D_MODEL), tok_spec(E_B), _const_spec((1, D_MODEL)), _const_spec(wg.shape),
            _const_spec(woutb.shape), _const_spec((1, D_MODEL)),
        ],
        out_specs=tok_spec(D_MODEL),
        out_shape=jax.ShapeDtypeStruct((bsz, seq, D_MODEL), F32),
        compiler_params=pltpu.CompilerParams(
            dimension_semantics=("arbitrary", "arbitrary"),
            vmem_limit_bytes=VMEM_LIMIT_BYTES),
        name="output_kernel",
    )(x1, o, row(b_norm[0]), wg, woutb, row(final_norm))
    return out
```

```python
import math
from functools import partial

import jax
import jax.numpy as jnp
from jax import lax
from jax.experimental import pallas as pl
from jax.experimental.pallas import tpu as pltpu

D_MODEL = 1024
E_A = D_MODEL
CONV_WIDTH = 3
N_HEADS = 8
QK_NOPE = 64
QK_ROPE = 32
V_HEAD = 64
KV_RANK = 256
Q_RANK = 384
E_B = N_HEADS * V_HEAD
ROPE_THETA = 10000.0
SOFTMAX_SCALE = 1.0 / math.sqrt(QK_NOPE + QK_ROPE)
EPS = 1e-6

LANES = 128
HEAD_PACK = LANES
BF16_SUBLANES = 16
V_PACK = V_HEAD + BF16_SUBLANES
TOKEN_TILE = 512
OUTPUT_TILE = 1024
CONV_CHUNK = 256
N_CONV_CHUNKS = E_A // CONV_CHUNK
CARRY_ROWS = 8
ATT_BLOCK = 512
ATT_QCHUNK = 256
HEADS_PER_STEP = 4
QK_LOOKAHEAD = 2
MASK_VALUE = -1e30
VMEM_LIMIT_BYTES = 56 * 1024 * 1024

BF16 = jnp.bfloat16
F32 = jnp.float32


def _rms_scale(x):
    return lax.rsqrt(jnp.mean(x * x, axis=-1, keepdims=True) + EPS)


def _dot(a, b):
    return jnp.dot(a, b, preferred_element_type=F32)


_NT_DIMS = (((1,), (1,)), ((), ()))


def _token_kernel(x_ref, cc_ref, ss_ref, anorm_ref, win_ref, conv_ref, wouta_ref,
                  kvnorm_ref, wdkv_ref, ckvnorm_ref, wkn_ref, wvt_ref, bnorm_ref, wcq_ref,
                  qnorm_ref, wq_ref,
                  x1_ref, q_ref, k_ref, vt_ref,
                  vbuf_ref, y_ref):
    tm = TOKEN_TILE
    cw = CONV_CHUNK

    @pl.when(pl.program_id(1) == 0)
    def _():
        vbuf_ref[0:CARRY_ROWS, :] = jnp.zeros((CARRY_ROWS, E_A), F32)

    x = x_ref[...]
    h = (x * _rms_scale(x) * anorm_ref[...]).astype(BF16)

    for j in range(N_CONV_CHUNKS):
        cols = slice(j * cw, (j + 1) * cw)
        pj = _dot(h, win_ref[:, j * 4 * cw:(j + 1) * 4 * cw])
        b = pj[:, 0:cw]
        c = pj[:, cw:2 * cw]
        u = pj[:, 2 * cw:3 * cw]
        g = pj[:, 3 * cw:4 * cw]
        v = c * u
        vbuf_ref[CARRY_ROWS:CARRY_ROWS + tm, cols] = v
        vm1 = vbuf_ref[CARRY_ROWS - 1:CARRY_ROWS - 1 + tm, cols]
        vm2 = vbuf_ref[CARRY_ROWS - 2:CARRY_ROWS - 2 + tm, cols]
        conv = conv_ref[0:1, cols] * vm2 + conv_ref[1:2, cols] * vm1 + conv_ref[2:3, cols] * v
        y = (g * jax.nn.sigmoid(g)) * b * conv
        y_ref[:, cols] = y.astype(BF16)

    vbuf_ref[0:CARRY_ROWS, :] = vbuf_ref[tm:tm + CARRY_ROWS, :]

    x1 = x + _dot(y_ref[...], wouta_ref[...])
    x1_ref[...] = x1

    xn = x1 * _rms_scale(x1)
    hkv = (xn * kvnorm_ref[...]).astype(BF16)
    hb = (xn * bnorm_ref[...]).astype(BF16)

    reps = LANES // QK_ROPE
    cc = jnp.concatenate([cc_ref[...]] * reps, axis=0).T
    ss = jnp.concatenate([ss_ref[...]] * reps, axis=0).T

    ckr = _dot(hkv, wdkv_ref[...])
    ckv = ckr[:, 0:KV_RANK]
    rope_k = ckr[:, KV_RANK:KV_RANK + LANES] * cc + ckr[:, KV_RANK + LANES:KV_RANK + 2 * LANES] * ss
    ckvn = (ckv * _rms_scale(ckv) * ckvnorm_ref[...]).astype(BF16)
    kn = _dot(ckvn, wkn_ref[...])
    vt = lax.dot_general(wvt_ref[...], ckvn, _NT_DIMS, preferred_element_type=F32)
    head_row = lax.broadcasted_iota(jnp.int32, (N_HEADS * V_PACK, 1), 0) % V_PACK
    vt_ref[...] = (vt + jnp.where(head_row == V_HEAD, 1.0, 0.0)).astype(BF16)

    lane = lax.broadcasted_iota(jnp.int32, (1, LANES), 1)

    cq = _dot(hb, wcq_ref[...])
    cqn = (cq * _rms_scale(cq) * qnorm_ref[...]).astype(BF16)
    q = _dot(cqn, wq_ref[...])
    qscale = SOFTMAX_SCALE * math.log2(math.e)
    tq = jnp.where(lane < QK_NOPE, 1.0, jnp.where(lane < QK_NOPE + QK_ROPE, cc, ss)) * qscale

    for hd in range(N_HEADS):
        hc = slice(hd * HEAD_PACK, (hd + 1) * HEAD_PACK)
        k_ref[:, hc] = (kn[:, hc] + rope_k).astype(BF16)
        q_ref[:, hc] = (q[:, hc] * tq).astype(BF16)


def _attention_kernel(q_ref, k_ref, vt_ref, o_ref, m_ref, alpha_ref, acc_ref, *st_refs):
    blk = ATT_BLOCK
    cw = ATT_QCHUNK
    nblk = q_ref.shape[0] // blk
    chains = [(hd, c) for hd in range(HEADS_PER_STEP) for c in range(blk // cw)]
    assert len(st_refs) == len(chains)

    def head_cols(i):
        hd = chains[i][0]
        return slice(hd * HEAD_PACK, (hd + 1) * HEAD_PACK)

    def state(ref, i):
        hd, c = chains[i]
        return ref.at[hd, :, c * cw:(c + 1) * cw]

    def qk(i, qi, ki, diagonal):
        c = chains[i][1]
        rows = (c + 1) * cw if diagonal else blk
        q = q_ref[pl.ds(pl.multiple_of(qi * blk + c * cw, cw), cw), head_cols(i)]
        k = k_ref[pl.ds(pl.multiple_of(ki * blk, blk), rows), head_cols(i)]
        return lax.dot_general(k, q, _NT_DIMS, preferred_element_type=F32)

    def stage_scores(i, s, diagonal):
        st = st_refs[i]
        if diagonal:
            c = chains[i][1]
            lo, hi = c * cw, (c + 1) * cw
            kv_idx = lax.broadcasted_iota(jnp.int32, (cw, cw), 0)
            q_idx = lax.broadcasted_iota(jnp.int32, (cw, cw), 1)
            tri = jnp.where(kv_idx <= q_idx, s[lo:hi], MASK_VALUE)
            m_new = jnp.max(tri, axis=0, keepdims=True)
            st[lo:hi, :] = tri
            if lo > 0:
                m_new = jnp.maximum(m_new, jnp.max(s[0:lo], axis=0, keepdims=True))
                st[0:lo, :] = s[0:lo]
            if hi < blk:
                st[hi:blk, :] = jnp.full((blk - hi, cw), MASK_VALUE, F32)
            m_old = jnp.full((1, cw), MASK_VALUE, F32)
        else:
            m_old = state(m_ref, i)[...]
            m_new = jnp.maximum(m_old, jnp.max(s, axis=0, keepdims=True))
            st[...] = s
        state(alpha_ref, i)[...] = jnp.exp2(m_old - m_new)
        state(m_ref, i)[...] = m_new

    def accumulate(i, ki):
        hd = chains[i][0]
        vt = vt_ref[ki, hd * V_PACK:(hd + 1) * V_PACK, :]
        p = jnp.exp2(st_refs[i][...] - state(m_ref, i)[...]).astype(BF16)
        acc = state(acc_ref, i)
        acc[...] = state(alpha_ref, i)[...] * acc[...] + _dot(vt, p)

    def step(qi, ki, diagonal, staged_ki):
        s = {j: qk(j, qi, ki, diagonal) for j in range(QK_LOOKAHEAD)}
        for i in range(len(chains)):
            if i + QK_LOOKAHEAD < len(chains):
                s[i + QK_LOOKAHEAD] = qk(i + QK_LOOKAHEAD, qi, ki, diagonal)
            accumulate(i, staged_ki)
            stage_scores(i, s.pop(i), diagonal)

    def finalize(qi):
        o_t = jnp.concatenate([acc_ref[hd, 0:V_HEAD, :] / acc_ref[hd, V_HEAD:V_HEAD + 1, :]
                               for hd in range(HEADS_PER_STEP)], axis=0)
        o_ref[pl.ds(pl.multiple_of(qi * blk, blk), blk), :] = o_t.T.astype(o_ref.dtype)

    acc_ref[...] = jnp.zeros(acc_ref.shape, F32)
    for i in range(len(chains)):
        stage_scores(i, qk(i, 0, 0, True), True)

    def q_block(qi, staged_ki):
        step(qi, qi, True, staged_ki)
        finalize(qi - 1)

        def kv_block(ki, staged):
            step(qi, ki, False, staged)
            return ki

        return lax.fori_loop(0, qi, kv_block, qi)

    staged_ki = lax.fori_loop(1, nblk, q_block, 0)
    for i in range(len(chains)):
        accumulate(i, staged_ki)
    finalize(nblk - 1)


def _output_kernel(x1_ref, o_ref, bnorm_ref, wg_ref, woutb_ref, fnorm_ref, out_ref):
    x1 = x1_ref[...]
    hb = (x1 * _rms_scale(x1) * bnorm_ref[...]).astype(BF16)
    g = _dot(hb, wg_ref[...])
    y = (o_ref[...].astype(F32) * (g * jax.nn.sigmoid(g))).astype(BF16)
    x2 = x1 + _dot(y, woutb_ref[...])
    out_ref[...] = x2 * _rms_scale(x2) * fnorm_ref[...]


def _const_spec(shape):
    return pl.BlockSpec(shape, lambda *_: (0,) * len(shape))


def _rope_tables(positions):
    inv_freq = ROPE_THETA ** (-jnp.arange(0, QK_ROPE, 2, dtype=F32) / QK_ROPE)
    ang = positions.astype(F32)[:, None, :] * inv_freq[None, :, None]
    cos, sin = jnp.cos(ang), jnp.sin(ang)
    return jnp.concatenate([cos, cos], axis=1), jnp.concatenate([-sin, sin], axis=1)


def _swap_halves(w):
    half = w.shape[-1] // 2
    return jnp.concatenate([w[..., half:], w[..., :half]], axis=-1)


def _prepare_weights(a_w_in, w_dkv, w_ukv, b_w_in, b_w_uq):
    win = jnp.concatenate(
        [a_w_in[:, part * E_A + j * CONV_CHUNK:part * E_A + (j + 1) * CONV_CHUNK]
         for j in range(N_CONV_CHUNKS) for part in range(4)], axis=-1).astype(BF16)

    kr = w_dkv[:, KV_RANK:]
    zeros = jnp.zeros((D_MODEL, QK_NOPE), F32)
    wdkv = jnp.concatenate([w_dkv[:, :KV_RANK], zeros, kr, kr,
                            zeros, _swap_halves(kr), _swap_halves(kr)], axis=-1).astype(BF16)

    wk = w_ukv.reshape(KV_RANK, N_HEADS, QK_NOPE + V_HEAD)
    pad_k = jnp.zeros((KV_RANK, N_HEADS, HEAD_PACK - QK_NOPE), F32)
    wkn = jnp.concatenate([wk[..., :QK_NOPE], pad_k], axis=-1).reshape(KV_RANK, N_HEADS * HEAD_PACK)
    wkn = wkn.astype(BF16)
    pad_v = jnp.zeros((KV_RANK, N_HEADS, V_PACK - V_HEAD), F32)
    wvt = jnp.concatenate([wk[..., QK_NOPE:], pad_v], axis=-1).reshape(KV_RANK, N_HEADS * V_PACK)
    wvt = wvt.T.astype(BF16)

    wcq = b_w_in[:, :Q_RANK].astype(BF16)
    wg = b_w_in[:, Q_RANK:].astype(BF16)

    wq = b_w_uq.reshape(Q_RANK, N_HEADS, QK_NOPE + QK_ROPE)
    rope = wq[..., QK_NOPE:]
    wq = jnp.concatenate([wq[..., :QK_NOPE], rope, _swap_halves(rope)], axis=-1)
    wq = wq.reshape(Q_RANK, N_HEADS * HEAD_PACK).astype(BF16)
    return win, wdkv, wkn, wvt, wcq, wg, wq


def kernel(x, positions, a_norm, a_w_in, a_conv, a_w_out, kv_norm, w_dkv, ckv_norm, w_ukv,
           b_norm, b_w_in, b_q_norm, b_w_uq, b_w_out, final_norm):
    bsz, seq, d = x.shape
    assert d == D_MODEL and seq % TOKEN_TILE == 0 and seq % ATT_BLOCK == 0 and seq % OUTPUT_TILE == 0
    assert a_norm.shape[0] == 1 and b_norm.shape[0] == 1, "depth-2 trunk: one conv layer, one MLA layer"

    cc, ss = _rope_tables(positions)
    win, wdkv, wkn, wvt, wcq, wg, wq = _prepare_weights(a_w_in[0], w_dkv, w_ukv, b_w_in[0], b_w_uq[0])
    wouta = a_w_out[0].astype(BF16)
    woutb = b_w_out[0].astype(BF16)
    row = lambda g: g.reshape(1, -1).astype(F32)

    tm = TOKEN_TILE
    assert tm == ATT_BLOCK, "v'^T is written one attention kv block per token tile"
    n_tiles = seq // tm
    hp = N_HEADS * HEAD_PACK
    tok_spec = lambda width: pl.BlockSpec((None, tm, width), lambda b, s: (b, s, 0))
    rope_spec = pl.BlockSpec((None, QK_ROPE, tm), lambda b, s: (b, 0, s))

    x1, q, k, vt = pl.pallas_call(
        _token_kernel,
        grid=(bsz, n_tiles),
        in_specs=[
            tok_spec(D_MODEL), rope_spec, rope_spec,
            _const_spec((1, D_MODEL)), _const_spec(win.shape), _const_spec(a_conv.shape[1:]),
            _const_spec(wouta.shape), _const_spec((1, D_MODEL)), _const_spec(wdkv.shape),
            _const_spec((1, KV_RANK)), _const_spec(wkn.shape), _const_spec(wvt.shape),
            _const_spec((1, D_MODEL)),
            _const_spec(wcq.shape), _const_spec((1, Q_RANK)), _const_spec(wq.shape),
        ],
        out_specs=[tok_spec(D_MODEL), tok_spec(hp), tok_spec(hp),
                   pl.BlockSpec((None, None, N_HEADS * V_PACK, tm), lambda b, s: (b, s, 0, 0))],
        out_shape=[
            jax.ShapeDtypeStruct((bsz, seq, D_MODEL), F32),
            jax.ShapeDtypeStruct((bsz, seq, hp), BF16),
            jax.ShapeDtypeStruct((bsz, seq, hp), BF16),
            jax.ShapeDtypeStruct((bsz, n_tiles, N_HEADS * V_PACK, tm), BF16),
        ],
        scratch_shapes=[
            pltpu.VMEM((tm + CARRY_ROWS, E_A), F32),
            pltpu.VMEM((tm, E_A), BF16),
        ],
        compiler_params=pltpu.CompilerParams(
            dimension_semantics=("arbitrary", "arbitrary"),
            vmem_limit_bytes=VMEM_LIMIT_BYTES),
        name="token_kernel",
    )(x, cc, ss, row(a_norm[0]), win, a_conv[0], wouta, row(kv_norm), wdkv, row(ckv_norm),
      wkn, wvt, row(b_norm[0]), wcq, row(b_q_norm[0]), wq)

    hw = HEADS_PER_STEP * HEAD_PACK
    att_spec = pl.BlockSpec((None, seq, hw), lambda b, h: (b, 0, h))
    o = pl.pallas_call(
        _attention_kernel,
        grid=(bsz, N_HEADS // HEADS_PER_STEP),
        in_specs=[att_spec, att_spec,
                  pl.BlockSpec((None, n_tiles, HEADS_PER_STEP * V_PACK, ATT_BLOCK),
                               lambda b, h: (b, 0, h, 0))],
        out_specs=pl.BlockSpec((None, seq, HEADS_PER_STEP * V_HEAD), lambda b, h: (b, 0, h)),
        out_shape=jax.ShapeDtypeStruct((bsz, seq, E_B), BF16),
        scratch_shapes=[
            pltpu.VMEM((HEADS_PER_STEP, 1, ATT_BLOCK), F32),
            pltpu.VMEM((HEADS_PER_STEP, 1, ATT_BLOCK), F32),
            pltpu.VMEM((HEADS_PER_STEP, V_PACK, ATT_BLOCK), F32),
        ] + [pltpu.VMEM((ATT_BLOCK, ATT_QCHUNK), F32)
             for _ in range(HEADS_PER_STEP * (ATT_BLOCK // ATT_QCHUNK))],
        compiler_params=pltpu.CompilerParams(
            dimension_semantics=("arbitrary", "arbitrary"),
            vmem_limit_bytes=VMEM_LIMIT_BYTES),
        name="attention_kernel",
    )(q, k, vt)

    out_spec = lambda width: pl.BlockSpec((None, OUTPUT_TILE, width), lambda b, s: (b, s, 0))
    out = pl.pallas_call(
        _output_kernel,
        grid=(bsz, seq // OUTPUT_TILE),
        in_specs=[
            out_spec(D_MODEL), out_spec(E_B), _const_spec((1, D_MODEL)), _const_spec(wg.shape),
            _const_spec(woutb.shape), _const_spec((1, D_MODEL)),
        ],
        out_specs=out_spec(D_MODEL),
        out_shape=jax.ShapeDtypeStruct((bsz, seq, D_MODEL), F32),
        compiler_params=pltpu.CompilerParams(
            dimension_semantics=("arbitrary", "arbitrary"),
            vmem_limit_bytes=VMEM_LIMIT_BYTES),
        name="output_kernel",
    )(x1, o, row(b_norm[0]), wg, woutb, row(final_norm))
    return out
```

```python
import math
from functools import partial

import jax
import jax.numpy as jnp
from jax import lax
from jax.experimental import pallas as pl
from jax.experimental.pallas import tpu as pltpu

D_MODEL = 1024
E_A = D_MODEL
CONV_WIDTH = 3
N_HEADS = 8
QK_NOPE = 64
QK_ROPE = 32
V_HEAD = 64
KV_RANK = 256
Q_RANK = 384
E_B = N_HEADS * V_HEAD
ROPE_THETA = 10000.0
SOFTMAX_SCALE = 1.0 / math.sqrt(QK_NOPE + QK_ROPE)
EPS = 1e-6

LANES = 128
HEAD_PACK = LANES
BF16_SUBLANES = 16
V_PACK = V_HEAD + BF16_SUBLANES
TOKEN_TILE = 512
OUTPUT_TILE = 1024
CONV_CHUNK = 256
N_CONV_CHUNKS = E_A // CONV_CHUNK
CARRY_ROWS = 8
ATT_BLOCK = 512
ATT_QCHUNK = 256
HEADS_PER_STEP = 4
ATT_KV_UNROLL = 2
QK_LOOKAHEAD = 2
MASK_VALUE = -1e30
VMEM_LIMIT_BYTES = 56 * 1024 * 1024

BF16 = jnp.bfloat16
F32 = jnp.float32


def _rms_scale(x):
    return lax.rsqrt(jnp.mean(x * x, axis=-1, keepdims=True) + EPS)


def _dot(a, b):
    return jnp.dot(a, b, preferred_element_type=F32)


_NT_DIMS = (((1,), (1,)), ((), ()))


def _token_kernel(x_ref, cc_ref, ss_ref, anorm_ref, win_ref, conv_ref, wouta_ref,
                  kvnorm_ref, wdkv_ref, ckvnorm_ref, wkn_ref, wvt_ref, bnorm_ref, wcq_ref,
                  qnorm_ref, wq_ref,
                  x1_ref, q_ref, k_ref, vt_ref,
                  vbuf_ref, y_ref):
    tm = TOKEN_TILE
    cw = CONV_CHUNK

    @pl.when(pl.program_id(1) == 0)
    def _():
        vbuf_ref[0:CARRY_ROWS, :] = jnp.zeros((CARRY_ROWS, E_A), F32)

    x = x_ref[...]
    h = (x * _rms_scale(x) * anorm_ref[...]).astype(BF16)

    for j in range(N_CONV_CHUNKS):
        cols = slice(j * cw, (j + 1) * cw)
        pj = _dot(h, win_ref[:, j * 4 * cw:(j + 1) * 4 * cw])
        b = pj[:, 0:cw]
        c = pj[:, cw:2 * cw]
        u = pj[:, 2 * cw:3 * cw]
        g = pj[:, 3 * cw:4 * cw]
        v = c * u
        vbuf_ref[CARRY_ROWS:CARRY_ROWS + tm, cols] = v
        vm1 = vbuf_ref[CARRY_ROWS - 1:CARRY_ROWS - 1 + tm, cols]
        vm2 = vbuf_ref[CARRY_ROWS - 2:CARRY_ROWS - 2 + tm, cols]
        conv = conv_ref[0:1, cols] * vm2 + conv_ref[1:2, cols] * vm1 + conv_ref[2:3, cols] * v
        y = (g * jax.nn.sigmoid(g)) * b * conv
        y_ref[:, cols] = y.astype(BF16)

    vbuf_ref[0:CARRY_ROWS, :] = vbuf_ref[tm:tm + CARRY_ROWS, :]

    x1 = x + _dot(y_ref[...], wouta_ref[...])
    x1_ref[...] = x1

    xn = x1 * _rms_scale(x1)
    hkv = (xn * kvnorm_ref[...]).astype(BF16)
    hb = (xn * bnorm_ref[...]).astype(BF16)

    reps = LANES // QK_ROPE
    cc = jnp.concatenate([cc_ref[...]] * reps, axis=0).T
    ss = jnp.concatenate([ss_ref[...]] * reps, axis=0).T

    ckr = _dot(hkv, wdkv_ref[...])
    ckv = ckr[:, 0:KV_RANK]
    rope_k = ckr[:, KV_RANK:KV_RANK + LANES] * cc + ckr[:, KV_RANK + LANES:KV_RANK + 2 * LANES] * ss
    ckvn = (ckv * _rms_scale(ckv) * ckvnorm_ref[...]).astype(BF16)
    kn = _dot(ckvn, wkn_ref[...])
    vt = lax.dot_general(wvt_ref[...], ckvn, _NT_DIMS, preferred_element_type=F32)
    head_row = lax.broadcasted_iota(jnp.int32, (N_HEADS * V_PACK, 1), 0) % V_PACK
    vt_ref[...] = (vt + jnp.where(head_row == V_HEAD, 1.0, 0.0)).astype(BF16)

    lane = lax.broadcasted_iota(jnp.int32, (1, LANES), 1)

    cq = _dot(hb, wcq_ref[...])
    cqn = (cq * _rms_scale(cq) * qnorm_ref[...]).astype(BF16)
    q = _dot(cqn, wq_ref[...])
    qscale = SOFTMAX_SCALE * math.log2(math.e)
    tq = jnp.where(lane < QK_NOPE, 1.0, jnp.where(lane < QK_NOPE + QK_ROPE, cc, ss)) * qscale

    for hd in range(N_HEADS):
        hc = slice(hd * HEAD_PACK, (hd + 1) * HEAD_PACK)
        k_ref[:, hc] = (kn[:, hc] + rope_k).astype(BF16)
        q_ref[:, hc] = (q[:, hc] * tq).astype(BF16)


def _attention_kernel(q_ref, k_ref, vt_ref, o_ref, m_ref, alpha_ref, acc_ref, *st_refs):
    blk = ATT_BLOCK
    cw = ATT_QCHUNK
    nblk = q_ref.shape[0] // blk
    chains = [(hd, c) for hd in range(HEADS_PER_STEP) for c in range(blk // cw)]
    assert len(st_refs) == len(chains)

    def head_cols(i):
        hd = chains[i][0]
        return slice(hd * HEAD_PACK, (hd + 1) * HEAD_PACK)

    def state(ref, i):
        hd, c = chains[i]
        return ref.at[hd, :, c * cw:(c + 1) * cw]

    def qk(i, qi, ki, diagonal):
        c = chains[i][1]
        rows = (c + 1) * cw if diagonal else blk
        q = q_ref[pl.ds(pl.multiple_of(qi * blk + c * cw, cw), cw), head_cols(i)]
        k = k_ref[pl.ds(pl.multiple_of(ki * blk, blk), rows), head_cols(i)]
        return lax.dot_general(k, q, _NT_DIMS, preferred_element_type=F32)

    def stage_scores(i, s, diagonal):
        st = st_refs[i]
        if diagonal:
            c = chains[i][1]
            lo, hi = c * cw, (c + 1) * cw
            kv_idx = lax.broadcasted_iota(jnp.int32, (cw, cw), 0)
            q_idx = lax.broadcasted_iota(jnp.int32, (cw, cw), 1)
            tri = jnp.where(kv_idx <= q_idx, s[lo:hi], MASK_VALUE)
            m_new = jnp.max(tri, axis=0, keepdims=True)
            st[lo:hi, :] = tri
            if lo > 0:
                m_new = jnp.maximum(m_new, jnp.max(s[0:lo], axis=0, keepdims=True))
                st[0:lo, :] = s[0:lo]
            if hi < blk:
                st[hi:blk, :] = jnp.full((blk - hi, cw), MASK_VALUE, F32)
            m_old = jnp.full((1, cw), MASK_VALUE, F32)
        else:
            m_old = state(m_ref, i)[...]
            m_new = jnp.maximum(m_old, jnp.max(s, axis=0, keepdims=True))
            st[...] = s
        state(alpha_ref, i)[...] = jnp.exp2(m_old - m_new)
        state(m_ref, i)[...] = m_new

    def accumulate(i, ki):
        hd = chains[i][0]
        vt = vt_ref[ki, hd * V_PACK:(hd + 1) * V_PACK, :]
        p = jnp.exp2(st_refs[i][...] - state(m_ref, i)[...]).astype(BF16)
        acc = state(acc_ref, i)
        acc[...] = state(alpha_ref, i)[...] * acc[...] + _dot(vt, p)

    def step(qi, ki, diagonal, staged_ki):
        s = {j: qk(j, qi, ki, diagonal) for j in range(QK_LOOKAHEAD)}
        for i in range(len(chains)):
            if i + QK_LOOKAHEAD < len(chains):
                s[i + QK_LOOKAHEAD] = qk(i + QK_LOOKAHEAD, qi, ki, diagonal)
            accumulate(i, staged_ki)
            stage_scores(i, s.pop(i), diagonal)

    def finalize(qi):
        o_t = jnp.concatenate([acc_ref[hd, 0:V_HEAD, :] / acc_ref[hd, V_HEAD:V_HEAD + 1, :]
                               for hd in range(HEADS_PER_STEP)], axis=0)
        o_ref[pl.ds(pl.multiple_of(qi * blk, blk), blk), :] = o_t.T.astype(o_ref.dtype)

    acc_ref[...] = jnp.zeros(acc_ref.shape, F32)
    for i in range(len(chains)):
        stage_scores(i, qk(i, 0, 0, True), True)

    def q_block(qi, staged_ki):
        step(qi, qi, True, staged_ki)
        finalize(qi - 1)
        step(qi, 0, False, qi)

        rest = qi - 1
        unroll = ATT_KV_UNROLL

        def kv_group(j, staged):
            for u in range(unroll):
                ki = unroll * j + u + 1
                step(qi, ki, False, staged)
                staged = ki
            return staged

        n_groups = lax.shift_right_logical(rest, unroll.bit_length() - 1)
        staged = lax.fori_loop(0, n_groups, kv_group, 0)

        def kv_single(ki, staged):
            step(qi, ki, False, staged)
            return ki

        return lax.fori_loop(n_groups * unroll + 1, qi, kv_single, staged)

    staged_ki = lax.fori_loop(1, nblk, q_block, 0)
    for i in range(len(chains)):
        accumulate(i, staged_ki)
    finalize(nblk - 1)


def _output_kernel(x1_ref, o_ref, bnorm_ref, wg_ref, woutb_ref, fnorm_ref, out_ref):
    x1 = x1_ref[...]
    hb = (x1 * _rms_scale(x1) * bnorm_ref[...]).astype(BF16)
    g = _dot(hb, wg_ref[...])
    y = (o_ref[...].astype(F32) * (g * jax.nn.sigmoid(g))).astype(BF16)
    x2 = x1 + _dot(y, woutb_ref[...])
    out_ref[...] = x2 * _rms_scale(x2) * fnorm_ref[...]


def _const_spec(shape):
    return pl.BlockSpec(shape, lambda *_: (0,) * len(shape))


def _rope_tables(positions):
    inv_freq = ROPE_THETA ** (-jnp.arange(0, QK_ROPE, 2, dtype=F32) / QK_ROPE)
    ang = positions.astype(F32)[:, None, :] * inv_freq[None, :, None]
    cos, sin = jnp.cos(ang), jnp.sin(ang)
    return jnp.concatenate([cos, cos], axis=1), jnp.concatenate([-sin, sin], axis=1)


def _swap_halves(w):
    half = w.shape[-1] // 2
    return jnp.concatenate([w[..., half:], w[..., :half]], axis=-1)


def _prepare_weights(a_w_in, w_dkv, w_ukv, b_w_in, b_w_uq):
    win = jnp.concatenate(
        [a_w_in[:, part * E_A + j * CONV_CHUNK:part * E_A + (j + 1) * CONV_CHUNK]
         for j in range(N_CONV_CHUNKS) for part in range(4)], axis=-1).astype(BF16)

    kr = w_dkv[:, KV_RANK:]
    zeros = jnp.zeros((D_MODEL, QK_NOPE), F32)
    wdkv = jnp.concatenate([w_dkv[:, :KV_RANK], zeros, kr, kr,
                            zeros, _swap_halves(kr), _swap_halves(kr)], axis=-1).astype(BF16)

    wk = w_ukv.reshape(KV_RANK, N_HEADS, QK_NOPE + V_HEAD)
    pad_k = jnp.zeros((KV_RANK, N_HEADS, HEAD_PACK - QK_NOPE), F32)
    wkn = jnp.concatenate([wk[..., :QK_NOPE], pad_k], axis=-1).reshape(KV_RANK, N_HEADS * HEAD_PACK)
    wkn = wkn.astype(BF16)
    pad_v = jnp.zeros((KV_RANK, N_HEADS, V_PACK - V_HEAD), F32)
    wvt = jnp.concatenate([wk[..., QK_NOPE:], pad_v], axis=-1).reshape(KV_RANK, N_HEADS * V_PACK)
    wvt = wvt.T.astype(BF16)

    wcq = b_w_in[:, :Q_RANK].astype(BF16)
    wg = b_w_in[:, Q_RANK:].astype(BF16)

    wq = b_w_uq.reshape(Q_RANK, N_HEADS, QK_NOPE + QK_ROPE)
    rope = wq[..., QK_NOPE:]
    wq = jnp.concatenate([wq[..., :QK_NOPE], rope, _swap_halves(rope)], axis=-1)
    wq = wq.reshape(Q_RANK, N_HEADS * HEAD_PACK).astype(BF16)
    return win, wdkv, wkn, wvt, wcq, wg, wq


def kernel(x, positions, a_norm, a_w_in, a_conv, a_w_out, kv_norm, w_dkv, ckv_norm, w_ukv,
           b_norm, b_w_in, b_q_norm, b_w_uq, b_w_out, final_norm):
    bsz, seq, d = x.shape
    assert d == D_MODEL and seq % TOKEN_TILE == 0 and seq % ATT_BLOCK == 0 and seq % OUTPUT_TILE == 0
    assert a_norm.shape[0] == 1 and b_norm.shape[0] == 1, "depth-2 trunk: one conv layer, one MLA layer"

    cc, ss = _rope_tables(positions)
    win, wdkv, wkn, wvt, wcq, wg, wq = _prepare_weights(a_w_in[0], w_dkv, w_ukv, b_w_in[0], b_w_uq[0])
    wouta = a_w_out[0].astype(BF16)
    woutb = b_w_out[0].astype(BF16)
    row = lambda g: g.reshape(1, -1).astype(F32)

    tm = TOKEN_TILE
    assert tm == ATT_BLOCK, "v'^T is written one attention kv block per token tile"
    n_tiles = seq // tm
    hp = N_HEADS * HEAD_PACK
    tok_spec = lambda width: pl.BlockSpec((None, tm, width), lambda b, s: (b, s, 0))
    rope_spec = pl.BlockSpec((None, QK_ROPE, tm), lambda b, s: (b, 0, s))

    x1, q, k, vt = pl.pallas_call(
        _token_kernel,
        grid=(bsz, n_tiles),
        in_specs=[
            tok_spec(D_MODEL), rope_spec, rope_spec,
            _const_spec((1, D_MODEL)), _const_spec(win.shape), _const_spec(a_conv.shape[1:]),
            _const_spec(wouta.shape), _const_spec((1, D_MODEL)), _const_spec(wdkv.shape),
            _const_spec((1, KV_RANK)), _const_spec(wkn.shape), _const_spec(wvt.shape),
            _const_spec((1, D_MODEL)),
            _const_spec(wcq.shape), _const_spec((1, Q_RANK)), _const_spec(wq.shape),
        ],
        out_specs=[tok_spec(D_MODEL), tok_spec(hp), tok_spec(hp),
                   pl.BlockSpec((None, None, N_HEADS * V_PACK, tm), lambda b, s: (b, s, 0, 0))],
        out_shape=[
            jax.ShapeDtypeStruct((bsz, seq, D_MODEL), F32),
            jax.ShapeDtypeStruct((bsz, seq, hp), BF16),
            jax.ShapeDtypeStruct((bsz, seq, hp), BF16),
            jax.ShapeDtypeStruct((bsz, n_tiles, N_HEADS * V_PACK, tm), BF16),
        ],
        scratch_shapes=[
            pltpu.VMEM((tm + CARRY_ROWS, E_A), F32),
            pltpu.VMEM((tm, E_A), BF16),
        ],
        compiler_params=pltpu.CompilerParams(
            dimension_semantics=("arbitrary", "arbitrary"),
            vmem_limit_bytes=VMEM_LIMIT_BYTES),
        name="token_kernel",
    )(x, cc, ss, row(a_norm[0]), win, a_conv[0], wouta, row(kv_norm), wdkv, row(ckv_norm),
      wkn, wvt, row(b_norm[0]), wcq, row(b_q_norm[0]), wq)

    hw = HEADS_PER_STEP * HEAD_PACK
    att_spec = pl.BlockSpec((None, seq, hw), lambda b, h: (b, 0, h))
    o = pl.pallas_call(
        _attention_kernel,
        grid=(bsz, N_HEADS // HEADS_PER_STEP),
        in_specs=[att_spec, att_spec,
                  pl.BlockSpec((None, n_tiles, HEADS_PER_STEP * V_PACK, ATT_BLOCK),
                               lambda b, h: (b, 0, h, 0))],
        out_specs=pl.BlockSpec((None, seq, HEADS_PER_STEP * V_HEAD), lambda b, h: (b, 0, h)),
        out_shape=jax.ShapeDtypeStruct((bsz, seq, E_B), BF16),
        scratch_shapes=[
            pltpu.VMEM((HEADS_PER_STEP, 1, ATT_BLOCK), F32),
            pltpu.VMEM((HEADS_PER_STEP, 1, ATT_BLOCK), F32),
            pltpu.VMEM((HEADS_PER_STEP, V_PACK, ATT_BLOCK), F32),
        ] + [pltpu.VMEM((ATT_BLOCK, ATT_QCHUNK), F32)
             for _ in range(HEADS_PER_STEP * (ATT_BLOCK // ATT_QCHUNK))],
        compiler_params=pltpu.CompilerParams(
            dimension_semantics=("arbitrary", "arbitrary"),
            vmem_limit_bytes=VMEM_LIMIT_BYTES),
        name="attention_kernel",
    )(q, k, vt)

    out_spec = lambda width: pl.BlockSpec((None, OUTPUT_TILE, width), lambda b, s: (b, s, 0))
    out = pl.pallas_call(
        _output_kernel,
        grid=(bsz, seq // OUTPUT_TILE),
        in_specs=[
            out_spec(D_MODEL), out_spec(E_B), _const_spec((1, D_MODEL)), _const_spec(wg.shape),
            _const_spec(woutb.shape), _const_spec((1, D_MODEL)),
        ],
        out_specs=out_spec(D_MODEL),
        out_shape=jax.ShapeDtypeStruct((bsz, seq, D_MODEL), F32),
        compiler_params=pltpu.CompilerParams(
            dimension_semantics=("arbitrary", "arbitrary"),
            vmem_limit_bytes=VMEM_LIMIT_BYTES),
        name="output_kernel",
    )(x1, o, row(b_norm[0]), wg, woutb, row(final_norm))
    return out
```

```python
import math
from functools import partial

import jax
import jax.numpy as jnp
from jax import lax
from jax.experimental import pallas as pl
from jax.experimental.pallas import tpu as pltpu

D_MODEL = 1024
E_A = D_MODEL
CONV_WIDTH = 3
N_HEADS = 8
QK_NOPE = 64
QK_ROPE = 32
V_HEAD = 64
KV_RANK = 256
Q_RANK = 384
E_B = N_HEADS * V_HEAD
ROPE_THETA = 10000.0
SOFTMAX_SCALE = 1.0 / math.sqrt(QK_NOPE + QK_ROPE)
EPS = 1e-6

LANES = 128
HEAD_PACK = LANES
BF16_SUBLANES = 16
V_PACK = V_HEAD + BF16_SUBLANES
TOKEN_TILE = 512
TOKEN_SUBTILES = 2
OUTPUT_TILE = 1024
OUTPUT_ROW_CHUNK = 256
CONV_CHUNK = 256
N_CONV_CHUNKS = E_A // CONV_CHUNK
CARRY_ROWS = 8
ATT_BLOCK = 512
ATT_QCHUNK = 256
HEADS_PER_STEP = 4
ATT_KV_UNROLL = 2
QK_LOOKAHEAD = 2
MASK_VALUE = -1e30
VMEM_LIMIT_BYTES = 56 * 1024 * 1024

BF16 = jnp.bfloat16
F32 = jnp.float32


def _rms_scale(x):
    return lax.rsqrt(jnp.mean(x * x, axis=-1, keepdims=True) + EPS)


def _dot(a, b):
    return jnp.dot(a, b, preferred_element_type=F32)


_NT_DIMS = (((1,), (1,)), ((), ()))


def _token_kernel(x_ref, cc_ref, ss_ref, anorm_ref, win_ref, conv_ref, wouta_ref,
                  kvnorm_ref, wdkv_ref, ckvnorm_ref, wkn_ref, wvt_ref, bnorm_ref, wcq_ref,
                  qnorm_ref, wq_ref,
                  x1_ref, q_ref, k_ref, vt_ref,
                  carry_ref, y_ref):
    tm = TOKEN_TILE

    @pl.when(pl.program_id(1) == 0)
    def _():
        carry_ref[...] = jnp.zeros((CARRY_ROWS, E_A), F32)

    def normed_input(sub):
        x = x_ref[sub * tm:(sub + 1) * tm, :]
        return x, (x * _rms_scale(x) * anorm_ref[...]).astype(BF16)

    x_h = normed_input(0)
    for sub in range(TOKEN_SUBTILES):
        prefetch = partial(normed_input, sub + 1) if sub + 1 < TOKEN_SUBTILES else (lambda: None)
        x_h = _token_subtile(sub * tm, sub, *x_h, prefetch, cc_ref, ss_ref, win_ref, conv_ref,
                             wouta_ref, kvnorm_ref, wdkv_ref, ckvnorm_ref, wkn_ref, wvt_ref,
                             bnorm_ref, wcq_ref, qnorm_ref, wq_ref, x1_ref, q_ref, k_ref, vt_ref,
                             carry_ref, y_ref)


def _token_subtile(r0, sub, x, h, prefetch_next, cc_ref, ss_ref, win_ref, conv_ref, wouta_ref,
                   kvnorm_ref, wdkv_ref, ckvnorm_ref, wkn_ref, wvt_ref, bnorm_ref, wcq_ref,
                   qnorm_ref, wq_ref, x1_ref, q_ref, k_ref, vt_ref, carry_ref, y_ref):
    tm = TOKEN_TILE
    cw = CONV_CHUNK
    rows = slice(r0, r0 + tm)

    in_proj = lambda j: _dot(h, win_ref[:, j * 4 * cw:(j + 1) * 4 * cw])
    pj_next = in_proj(0)
    for j in range(N_CONV_CHUNKS):
        cols = slice(j * cw, (j + 1) * cw)
        pj = pj_next
        if j + 1 < N_CONV_CHUNKS:
            pj_next = in_proj(j + 1)
        b = pj[:, 0:cw]
        c = pj[:, cw:2 * cw]
        u = pj[:, 2 * cw:3 * cw]
        g = pj[:, 3 * cw:4 * cw]
        v = c * u
        vv = jnp.concatenate([carry_ref[:, cols], v], axis=0)
        vm1 = pltpu.roll(vv, 1, axis=0)[CARRY_ROWS:]
        vm2 = pltpu.roll(vv, 2, axis=0)[CARRY_ROWS:]
        carry_ref[:, cols] = v[tm - CARRY_ROWS:tm]
        conv = conv_ref[0:1, cols] * vm2 + conv_ref[1:2, cols] * vm1 + conv_ref[2:3, cols] * v
        y = (g * jax.nn.sigmoid(g)) * b * conv
        y_ref[rows, cols] = y.astype(BF16)

    x1 = x + _dot(y_ref[rows, :], wouta_ref[...])
    x1_ref[rows, :] = x1

    xn = x1 * _rms_scale(x1)
    hkv = (xn * kvnorm_ref[...]).astype(BF16)
    hb = (xn * bnorm_ref[...]).astype(BF16)

    reps = LANES // QK_ROPE
    cc = jnp.concatenate([cc_ref[:, rows]] * reps, axis=0).T
    ss = jnp.concatenate([ss_ref[:, rows]] * reps, axis=0).T

    lane = lax.broadcasted_iota(jnp.int32, (1, LANES), 1)
    low_half = lane < LANES // 2

    ckr = _dot(hkv, wdkv_ref[...])
    ckv = ckr[:, 0:KV_RANK]
    rope_terms = ckr[:, KV_RANK:KV_RANK + LANES] * jnp.where(low_half, cc, ss)
    rope_k = rope_terms + pltpu.roll(rope_terms, LANES // 2, axis=1)
    ckvn = (ckv * _rms_scale(ckv) * ckvnorm_ref[...]).astype(BF16)
    kn = _dot(ckvn, wkn_ref[...])
    vt = lax.dot_general(wvt_ref[...], ckvn, _NT_DIMS, preferred_element_type=F32)
    head_row = lax.broadcasted_iota(jnp.int32, (N_HEADS * V_PACK, 1), 0) % V_PACK
    vt_ref[sub] = (vt + jnp.where(head_row == V_HEAD, 1.0, 0.0)).astype(BF16)

    next_x_h = prefetch_next()
    cq = _dot(hb, wcq_ref[...])
    cqn = (cq * _rms_scale(cq) * qnorm_ref[...]).astype(BF16)
    q = _dot(cqn, wq_ref[...])
    qscale = SOFTMAX_SCALE * math.log2(math.e)
    tq = jnp.where(lane < QK_NOPE, 1.0, jnp.where(lane < QK_NOPE + QK_ROPE, cc, ss)) * qscale

    for hd in range(N_HEADS):
        hc = slice(hd * HEAD_PACK, (hd + 1) * HEAD_PACK)
        kn_pair = kn[:, (hd // 2) * LANES:(hd // 2 + 1) * LANES]
        if hd % 2 == 1:
            kn_pair = pltpu.roll(kn_pair, LANES // 2, axis=1)
        k_ref[rows, hc] = jnp.where(low_half, kn_pair, rope_k).astype(BF16)
        q_ref[rows, hc] = (q[:, hc] * tq).astype(BF16)
    return next_x_h


def _attention_kernel(q_ref, k_ref, vt_ref, o_ref, m_ref, alpha_ref, acc_ref, *st_refs):
    blk = ATT_BLOCK
    cw = ATT_QCHUNK
    nblk = q_ref.shape[0] // blk
    chains = [(hd, c) for hd in range(HEADS_PER_STEP) for c in range(blk // cw)]
    assert len(st_refs) == len(chains)

    def head_cols(i):
        hd = chains[i][0]
        return slice(hd * HEAD_PACK, (hd + 1) * HEAD_PACK)

    def state(ref, i):
        hd, c = chains[i]
        return ref.at[hd, :, c * cw:(c + 1) * cw]

    def qk(i, qi, ki, diagonal):
        c = chains[i][1]
        rows = (c + 1) * cw if diagonal else blk
        q = q_ref[pl.ds(pl.multiple_of(qi * blk + c * cw, cw), cw), head_cols(i)]
        k = k_ref[pl.ds(pl.multiple_of(ki * blk, blk), rows), head_cols(i)]
        return lax.dot_general(k, q, _NT_DIMS, preferred_element_type=F32)

    def stage_scores(i, s, diagonal):
        st = st_refs[i]
        if diagonal:
            c = chains[i][1]
            lo, hi = c * cw, (c + 1) * cw
            kv_idx = lax.broadcasted_iota(jnp.int32, (cw, cw), 0)
            q_idx = lax.broadcasted_iota(jnp.int32, (cw, cw), 1)
            tri = jnp.where(kv_idx <= q_idx, s[lo:hi], MASK_VALUE)
            m_new = jnp.max(tri, axis=0, keepdims=True)
            st[lo:hi, :] = tri
            if lo > 0:
                m_new = jnp.maximum(m_new, jnp.max(s[0:lo], axis=0, keepdims=True))
                st[0:lo, :] = s[0:lo]
            if hi < blk:
                st[hi:blk, :] = jnp.full((blk - hi, cw), MASK_VALUE, F32)
            m_old = jnp.full((1, cw), MASK_VALUE, F32)
        else:
            m_old = state(m_ref, i)[...]
            m_new = jnp.maximum(m_old, jnp.max(s, axis=0, keepdims=True))
            st[...] = s
        state(alpha_ref, i)[...] = jnp.exp2(m_old - m_new)
        state(m_ref, i)[...] = m_new

    def accumulate(i, ki):
        hd = chains[i][0]
        vt = vt_ref[ki, hd * V_PACK:(hd + 1) * V_PACK, :]
        p = jnp.exp2(st_refs[i][...] - state(m_ref, i)[...]).astype(BF16)
        acc = state(acc_ref, i)
        acc[...] = state(alpha_ref, i)[...] * acc[...] + _dot(vt, p)

    def step(qi, ki, diagonal, staged_ki):
        s = {j: qk(j, qi, ki, diagonal) for j in range(QK_LOOKAHEAD)}
        for i in range(len(chains)):
            if i + QK_LOOKAHEAD < len(chains):
                s[i + QK_LOOKAHEAD] = qk(i + QK_LOOKAHEAD, qi, ki, diagonal)
            accumulate(i, staged_ki)
            stage_scores(i, s.pop(i), diagonal)

    def finalize(qi):
        o_t = jnp.concatenate([acc_ref[hd, 0:V_HEAD, :] / acc_ref[hd, V_HEAD:V_HEAD + 1, :]
                               for hd in range(HEADS_PER_STEP)], axis=0)
        o_ref[pl.ds(pl.multiple_of(qi * blk, blk), blk), :] = o_t.T.astype(o_ref.dtype)

    acc_ref[...] = jnp.zeros(acc_ref.shape, F32)
    for i in range(len(chains)):
        stage_scores(i, qk(i, 0, 0, True), True)

    def q_block(qi, staged_ki):
        step(qi, qi, True, staged_ki)
        finalize(qi - 1)
        step(qi, 0, False, qi)

        rest = qi - 1
        unroll = ATT_KV_UNROLL

        def kv_group(j, staged):
            for u in range(unroll):
                ki = unroll * j + u + 1
                step(qi, ki, False, staged)
                staged = ki
            return staged

        n_groups = lax.shift_right_logical(rest, unroll.bit_length() - 1)
        staged = lax.fori_loop(0, n_groups, kv_group, 0)

        def kv_single(ki, staged):
            step(qi, ki, False, staged)
            return ki

        return lax.fori_loop(n_groups * unroll + 1, qi, kv_single, staged)

    staged_ki = lax.fori_loop(1, nblk, q_block, 0)
    for i in range(len(chains)):
        accumulate(i, staged_ki)
    finalize(nblk - 1)


def _output_kernel(x1_ref, o_ref, bnorm_ref, wg_ref, woutb_ref, fnorm_ref, out_ref):
    rc = OUTPUT_ROW_CHUNK
    n_chunks = x1_ref.shape[0] // rc
    rows = lambda c: slice(c * rc, (c + 1) * rc)
    hb, g, y = {}, {}, {}

    def norm_in(c):
        x1 = x1_ref[rows(c), :]
        hb[c] = (x1 * _rms_scale(x1) * bnorm_ref[...]).astype(BF16)

    def gate_proj(c):
        g[c] = _dot(hb.pop(c), wg_ref[...])

    def gate(c):
        gc = g.pop(c)
        y[c] = (o_ref[rows(c), :].astype(F32) * (gc * jax.nn.sigmoid(gc))).astype(BF16)

    def out_proj_norm(c):
        x2 = x1_ref[rows(c), :] + _dot(y.pop(c), woutb_ref[...])
        out_ref[rows(c), :] = x2 * _rms_scale(x2) * fnorm_ref[...]

    stages = (norm_in, gate_proj, gate, out_proj_norm)
    for t in range(n_chunks + len(stages) - 1):
        for depth, stage in reversed(list(enumerate(stages))):
            if 0 <= t - depth < n_chunks:
                stage(t - depth)


def _const_spec(shape):
    return pl.BlockSpec(shape, lambda *_: (0,) * len(shape))


def _rope_tables(positions):
    inv_freq = ROPE_THETA ** (-jnp.arange(0, QK_ROPE, 2, dtype=F32) / QK_ROPE)
    ang = positions.astype(F32)[:, None, :] * inv_freq[None, :, None]
    cos, sin = jnp.cos(ang), jnp.sin(ang)
    return jnp.concatenate([cos, cos], axis=1), jnp.concatenate([-sin, sin], axis=1)


def _swap_halves(w):
    half = w.shape[-1] // 2
    return jnp.concatenate([w[..., half:], w[..., :half]], axis=-1)


def _prepare_weights(a_w_in, w_dkv, w_ukv, b_w_in, b_w_uq):
    win = jnp.concatenate(
        [a_w_in[:, part * E_A + j * CONV_CHUNK:part * E_A + (j + 1) * CONV_CHUNK]
         for j in range(N_CONV_CHUNKS) for part in range(4)], axis=-1).astype(BF16)

    kr = w_dkv[:, KV_RANK:]
    wdkv = jnp.concatenate([w_dkv[:, :KV_RANK], kr, kr, _swap_halves(kr), _swap_halves(kr)],
                           axis=-1).astype(BF16)

    wk = w_ukv.reshape(KV_RANK, N_HEADS, QK_NOPE + V_HEAD)
    wkn = wk[..., :QK_NOPE].reshape(KV_RANK, N_HEADS * QK_NOPE).astype(BF16)
    pad_v = jnp.zeros((KV_RANK, N_HEADS, V_PACK - V_HEAD), F32)
    wvt = jnp.concatenate([wk[..., QK_NOPE:], pad_v], axis=-1).reshape(KV_RANK, N_HEADS * V_PACK)
    wvt = wvt.T.astype(BF16)

    wcq = b_w_in[:, :Q_RANK].astype(BF16)
    wg = b_w_in[:, Q_RANK:].astype(BF16)

    wq = b_w_uq.reshape(Q_RANK, N_HEADS, QK_NOPE + QK_ROPE)
    rope = wq[..., QK_NOPE:]
    wq = jnp.concatenate([wq[..., :QK_NOPE], rope, _swap_halves(rope)], axis=-1)
    wq = wq.reshape(Q_RANK, N_HEADS * HEAD_PACK).astype(BF16)
    return win, wdkv, wkn, wvt, wcq, wg, wq


def kernel(x, positions, a_norm, a_w_in, a_conv, a_w_out, kv_norm, w_dkv, ckv_norm, w_ukv,
           b_norm, b_w_in, b_q_norm, b_w_uq, b_w_out, final_norm):
    bsz, seq, d = x.shape
    assert d == D_MODEL and seq % (TOKEN_SUBTILES * TOKEN_TILE) == 0
    assert seq % ATT_BLOCK == 0 and seq % OUTPUT_TILE == 0
    assert a_norm.shape[0] == 1 and b_norm.shape[0] == 1, "depth-2 trunk: one conv layer, one MLA layer"

    cc, ss = _rope_tables(positions)
    win, wdkv, wkn, wvt, wcq, wg, wq = _prepare_weights(a_w_in[0], w_dkv, w_ukv, b_w_in[0], b_w_uq[0])
    wouta = a_w_out[0].astype(BF16)
    woutb = b_w_out[0].astype(BF16)
    row = lambda g: g.reshape(1, -1).astype(F32)

    tm = TOKEN_TILE
    assert tm == ATT_BLOCK, "v'^T is written one attention kv block per token sub-tile"
    n_tiles = seq // tm
    step_rows = TOKEN_SUBTILES * tm
    hp = N_HEADS * HEAD_PACK
    tok_spec = lambda width: pl.BlockSpec((None, step_rows, width), lambda b, s: (b, s, 0))
    rope_spec = pl.BlockSpec((None, QK_ROPE, step_rows), lambda b, s: (b, 0, s))

    x1, q, k, vt = pl.pallas_call(
        _token_kernel,
        grid=(bsz, seq // step_rows),
        in_specs=[
            tok_spec(D_MODEL), rope_spec, rope_spec,
            _const_spec((1, D_MODEL)), _const_spec(win.shape), _const_spec(a_conv.shape[1:]),
            _const_spec(wouta.shape), _const_spec((1, D_MODEL)), _const_spec(wdkv.shape),
            _const_spec((1, KV_RANK)), _const_spec(wkn.shape), _const_spec(wvt.shape),
            _const_spec((1, D_MODEL)),
            _const_spec(wcq.shape), _const_spec((1, Q_RANK)), _const_spec(wq.shape),
        ],
        out_specs=[tok_spec(D_MODEL), tok_spec(hp), tok_spec(hp),
                   pl.BlockSpec((None, TOKEN_SUBTILES, N_HEADS * V_PACK, tm),
                                lambda b, s: (b, s, 0, 0))],
        out_shape=[
            jax.ShapeDtypeStruct((bsz, seq, D_MODEL), F32),
            jax.ShapeDtypeStruct((bsz, seq, hp), BF16),
            jax.ShapeDtypeStruct((bsz, seq, hp), BF16),
            jax.ShapeDtypeStruct((bsz, n_tiles, N_HEADS * V_PACK, tm), BF16),
        ],
        scratch_shapes=[
            pltpu.VMEM((CARRY_ROWS, E_A), F32),
            pltpu.VMEM((step_rows, E_A), BF16),
        ],
        compiler_params=pltpu.CompilerParams(
            dimension_semantics=("arbitrary", "arbitrary"),
            vmem_limit_bytes=VMEM_LIMIT_BYTES),
        name="token_kernel",
    )(x, cc, ss, row(a_norm[0]), win, a_conv[0], wouta, row(kv_norm), wdkv, row(ckv_norm),
      wkn, wvt, row(b_norm[0]), wcq, row(b_q_norm[0]), wq)

    hw = HEADS_PER_STEP * HEAD_PACK
    att_spec = pl.BlockSpec((None, seq, hw), lambda b, h: (b, 0, h))
    o = pl.pallas_call(
        _attention_kernel,
        grid=(bsz, N_HEADS // HEADS_PER_STEP),
        in_specs=[att_spec, att_spec,
                  pl.BlockSpec((None, n_tiles, HEADS_PER_STEP * V_PACK, ATT_BLOCK),
                               lambda b, h: (b, 0, h, 0))],
        out_specs=pl.BlockSpec((None, seq, HEADS_PER_STEP * V_HEAD), lambda b, h: (b, 0, h)),
        out_shape=jax.ShapeDtypeStruct((bsz, seq, E_B), BF16),
        scratch_shapes=[
            pltpu.VMEM((HEADS_PER_STEP, 1, ATT_BLOCK), F32),
            pltpu.VMEM((HEADS_PER_STEP, 1, ATT_BLOCK), F32),
            pltpu.VMEM((HEADS_PER_STEP, V_PACK, ATT_BLOCK), F32),
        ] + [pltpu.VMEM((ATT_BLOCK, ATT_QCHUNK), F32)
             for _ in range(HEADS_PER_STEP * (ATT_BLOCK // ATT_QCHUNK))],
        compiler_params=pltpu.CompilerParams(
            dimension_semantics=("arbitrary", "arbitrary"),
            vmem_limit_bytes=VMEM_LIMIT_BYTES),
        name="attention_kernel",
    )(q, k, vt)

    out_spec = lambda width: pl.BlockSpec((None, OUTPUT_TILE, width), lambda b, s: (b, s, 0))
    out = pl.pallas_call(
        _output_kernel,
        grid=(bsz, seq // OUTPUT_TILE),
        in_specs=[
            out_spec(D_MODEL), out_spec(E_B), _const_spec((1, D_MODEL)), _const_spec(wg.shape),
            _const_spec(woutb.shape), _const_spec((1, D_MODEL)),
        ],
        out_specs=out_spec(D_MODEL),
        out_shape=jax.ShapeDtypeStruct((bsz, seq, D_MODEL), F32),
        compiler_params=pltpu.CompilerParams(
            dimension_semantics=("arbitrary", "arbitrary"),
            vmem_limit_bytes=VMEM_LIMIT_BYTES),
        name="output_kernel",
    )(x1, o, row(b_norm[0]), wg, woutb, row(final_norm))
    return out
```

```python
import math
from functools import partial

import jax
import jax.numpy as jnp
from jax import lax
from jax.experimental import pallas as pl
from jax.experimental.pallas import tpu as pltpu

D_MODEL = 1024
E_A = D_MODEL
CONV_WIDTH = 3
N_HEADS = 8
QK_NOPE = 64
QK_ROPE = 32
V_HEAD = 64
KV_RANK = 256
Q_RANK = 384
E_B = N_HEADS * V_HEAD
ROPE_THETA = 10000.0
SOFTMAX_SCALE = 1.0 / math.sqrt(QK_NOPE + QK_ROPE)
EPS = 1e-6

LANES = 128
HEAD_PACK = LANES
BF16_SUBLANES = 16
V_PACK = V_HEAD + BF16_SUBLANES
TOKEN_TILE = 512
TOKEN_SUBTILES = 2
OUTPUT_TILE = 2048
OUTPUT_ROW_CHUNK = 256
CONV_CHUNK = 128
N_CONV_CHUNKS = E_A // CONV_CHUNK
CARRY_ROWS = 8
ATT_BLOCK = 512
ATT_QCHUNK = 256
HEADS_PER_STEP = 4
ATT_KV_UNROLL = 2
QK_LOOKAHEAD = 2
MASK_VALUE = -1e30
VMEM_LIMIT_BYTES = 56 * 1024 * 1024

BF16 = jnp.bfloat16
F32 = jnp.float32


def _rms_scale(x):
    return lax.rsqrt(jnp.mean(x * x, axis=-1, keepdims=True) + EPS)


def _dot(a, b):
    return jnp.dot(a, b, preferred_element_type=F32)


_NT_DIMS = (((1,), (1,)), ((), ()))


def _token_kernel(x_ref, cc_ref, ss_ref, anorm_ref, win_ref, conv_ref, wouta_ref,
                  kvnorm_ref, wdkv_ref, ckvnorm_ref, wkn_ref, wvt_ref, bnorm_ref, wcq_ref,
                  qnorm_ref, wq_ref,
                  x1_ref, q_ref, k_ref, vt_ref,
                  carry_ref, y_ref):
    tm = TOKEN_TILE

    @pl.when(pl.program_id(1) == 0)
    def _():
        carry_ref[...] = jnp.zeros((CARRY_ROWS, E_A), F32)

    def normed_input(sub):
        x = x_ref[sub * tm:(sub + 1) * tm, :]
        return x, (x * _rms_scale(x) * anorm_ref[...]).astype(BF16)

    x_h = normed_input(0)
    for sub in range(TOKEN_SUBTILES):
        prefetch = partial(normed_input, sub + 1) if sub + 1 < TOKEN_SUBTILES else (lambda: None)
        x_h = _token_subtile(sub * tm, sub, *x_h, prefetch, cc_ref, ss_ref, win_ref, conv_ref,
                             wouta_ref, kvnorm_ref, wdkv_ref, ckvnorm_ref, wkn_ref, wvt_ref,
                             bnorm_ref, wcq_ref, qnorm_ref, wq_ref, x1_ref, q_ref, k_ref, vt_ref,
                             carry_ref, y_ref)


def _token_subtile(r0, sub, x, h, prefetch_next, cc_ref, ss_ref, win_ref, conv_ref, wouta_ref,
                   kvnorm_ref, wdkv_ref, ckvnorm_ref, wkn_ref, wvt_ref, bnorm_ref, wcq_ref,
                   qnorm_ref, wq_ref, x1_ref, q_ref, k_ref, vt_ref, carry_ref, y_ref):
    tm = TOKEN_TILE
    cw = CONV_CHUNK
    rows = slice(r0, r0 + tm)

    in_proj = lambda j: _dot(h, win_ref[:, j * 4 * cw:(j + 1) * 4 * cw])
    pj_next = in_proj(0)
    for j in range(N_CONV_CHUNKS):
        cols = slice(j * cw, (j + 1) * cw)
        pj = pj_next
        if j + 1 < N_CONV_CHUNKS:
            pj_next = in_proj(j + 1)
        b = pj[:, 0:cw]
        c = pj[:, cw:2 * cw]
        u = pj[:, 2 * cw:3 * cw]
        g = pj[:, 3 * cw:4 * cw]
        v = c * u
        vv = jnp.concatenate([carry_ref[:, cols], v], axis=0)
        vm1 = pltpu.roll(vv, 1, axis=0)[CARRY_ROWS:]
        vm2 = pltpu.roll(vv, 2, axis=0)[CARRY_ROWS:]
        carry_ref[:, cols] = v[tm - CARRY_ROWS:tm]
        conv = conv_ref[0:1, cols] * vm2 + conv_ref[1:2, cols] * vm1 + conv_ref[2:3, cols] * v
        y = (g * jax.nn.sigmoid(g)) * b * conv
        y_ref[rows, cols] = y.astype(BF16)

    x1 = x + _dot(y_ref[rows, :], wouta_ref[...])
    x1_ref[rows, :] = x1

    xn = x1 * _rms_scale(x1)
    hkv = (xn * kvnorm_ref[...]).astype(BF16)
    hb = (xn * bnorm_ref[...]).astype(BF16)

    reps = LANES // QK_ROPE
    cc = jnp.concatenate([cc_ref[:, rows]] * reps, axis=0).T
    ss = jnp.concatenate([ss_ref[:, rows]] * reps, axis=0).T

    lane = lax.broadcasted_iota(jnp.int32, (1, LANES), 1)
    low_half = lane < LANES // 2

    ckr = _dot(hkv, wdkv_ref[...])
    ckv = ckr[:, 0:KV_RANK]
    rope_terms = ckr[:, KV_RANK:KV_RANK + LANES] * jnp.where(low_half, cc, ss)
    rope_k = rope_terms + pltpu.roll(rope_terms, LANES // 2, axis=1)
    ckvn = (ckv * _rms_scale(ckv) * ckvnorm_ref[...]).astype(BF16)
    kn = _dot(ckvn, wkn_ref[...])
    vt = lax.dot_general(wvt_ref[...], ckvn, _NT_DIMS, preferred_element_type=F32)
    head_row = lax.broadcasted_iota(jnp.int32, (N_HEADS * V_PACK, 1), 0) % V_PACK
    vt_ref[sub] = (vt + jnp.where(head_row == V_HEAD, 1.0, 0.0)).astype(BF16)

    next_x_h = prefetch_next()
    cq = _dot(hb, wcq_ref[...])
    cqn = (cq * _rms_scale(cq) * qnorm_ref[...]).astype(BF16)
    q = _dot(cqn, wq_ref[...])
    qscale = SOFTMAX_SCALE * math.log2(math.e)
    tq = jnp.where(lane < QK_NOPE, 1.0, jnp.where(lane < QK_NOPE + QK_ROPE, cc, ss)) * qscale

    for hd in range(N_HEADS):
        hc = slice(hd * HEAD_PACK, (hd + 1) * HEAD_PACK)
        kn_pair = kn[:, (hd // 2) * LANES:(hd // 2 + 1) * LANES]
        if hd % 2 == 1:
            kn_pair = pltpu.roll(kn_pair, LANES // 2, axis=1)
        k_ref[rows, hc] = jnp.where(low_half, kn_pair, rope_k).astype(BF16)
        q_ref[rows, hc] = (q[:, hc] * tq).astype(BF16)
    return next_x_h


def _attention_kernel(q_ref, k_ref, vt_ref, o_ref, m_ref, alpha_ref, acc_ref, *st_refs):
    blk = ATT_BLOCK
    cw = ATT_QCHUNK
    nblk = q_ref.shape[0] // blk
    chains = [(hd, c) for hd in range(HEADS_PER_STEP) for c in range(blk // cw)]
    assert len(st_refs) == len(chains)

    def head_cols(i):
        hd = chains[i][0]
        return slice(hd * HEAD_PACK, (hd + 1) * HEAD_PACK)

    def state(ref, i):
        hd, c = chains[i]
        return ref.at[hd, :, c * cw:(c + 1) * cw]

    def qk(i, qi, ki, diagonal):
        c = chains[i][1]
        rows = (c + 1) * cw if diagonal else blk
        q = q_ref[pl.ds(pl.multiple_of(qi * blk + c * cw, cw), cw), head_cols(i)]
        k = k_ref[pl.ds(pl.multiple_of(ki * blk, blk), rows), head_cols(i)]
        return lax.dot_general(k, q, _NT_DIMS, preferred_element_type=F32)

    def stage_scores(i, s, diagonal):
        st = st_refs[i]
        if diagonal:
            c = chains[i][1]
            lo, hi = c * cw, (c + 1) * cw
            kv_idx = lax.broadcasted_iota(jnp.int32, (cw, cw), 0)
            q_idx = lax.broadcasted_iota(jnp.int32, (cw, cw), 1)
            tri = jnp.where(kv_idx <= q_idx, s[lo:hi], MASK_VALUE)
            m_new = jnp.max(tri, axis=0, keepdims=True)
            st[lo:hi, :] = tri
            if lo > 0:
                m_new = jnp.maximum(m_new, jnp.max(s[0:lo], axis=0, keepdims=True))
                st[0:lo, :] = s[0:lo]
            if hi < blk:
                st[hi:blk, :] = jnp.full((blk - hi, cw), MASK_VALUE, F32)
            m_old = jnp.full((1, cw), MASK_VALUE, F32)
        else:
            m_old = state(m_ref, i)[...]
            m_new = jnp.maximum(m_old, jnp.max(s, axis=0, keepdims=True))
            st[...] = s
        state(alpha_ref, i)[...] = jnp.exp2(m_old - m_new)
        state(m_ref, i)[...] = m_new

    def accumulate(i, ki):
        hd = chains[i][0]
        vt = vt_ref[ki, hd * V_PACK:(hd + 1) * V_PACK, :]
        p = jnp.exp2(st_refs[i][...] - state(m_ref, i)[...]).astype(BF16)
        acc = state(acc_ref, i)
        acc[...] = state(alpha_ref, i)[...] * acc[...] + _dot(vt, p)

    def step(qi, ki, diagonal, staged_ki):
        s = {j: qk(j, qi, ki, diagonal) for j in range(QK_LOOKAHEAD)}
        for i in range(len(chains)):
            if i + QK_LOOKAHEAD < len(chains):
                s[i + QK_LOOKAHEAD] = qk(i + QK_LOOKAHEAD, qi, ki, diagonal)
            accumulate(i, staged_ki)
            stage_scores(i, s.pop(i), diagonal)

    def finalize(qi):
        o_t = jnp.concatenate([acc_ref[hd, 0:V_HEAD, :] / acc_ref[hd, V_HEAD:V_HEAD + 1, :]
                               for hd in range(HEADS_PER_STEP)], axis=0)
        o_ref[pl.ds(pl.multiple_of(qi * blk, blk), blk), :] = o_t.T.astype(o_ref.dtype)

    acc_ref[...] = jnp.zeros(acc_ref.shape, F32)
    for i in range(len(chains)):
        stage_scores(i, qk(i, 0, 0, True), True)

    def q_block(qi, staged_ki):
        step(qi, qi, True, staged_ki)
        finalize(qi - 1)
        step(qi, 0, False, qi)

        rest = qi - 1
        unroll = ATT_KV_UNROLL

        def kv_group(j, staged):
            for u in range(unroll):
                ki = unroll * j + u + 1
                step(qi, ki, False, staged)
                staged = ki
            return staged

        n_groups = lax.shift_right_logical(rest, unroll.bit_length() - 1)
        staged = lax.fori_loop(0, n_groups, kv_group, 0)

        def kv_single(ki, staged):
            step(qi, ki, False, staged)
            return ki

        return lax.fori_loop(n_groups * unroll + 1, qi, kv_single, staged)

    staged_ki = lax.fori_loop(1, nblk, q_block, 0)
    for i in range(len(chains)):
        accumulate(i, staged_ki)
    finalize(nblk - 1)


def _output_kernel(x1_ref, o_ref, bnorm_ref, wg_ref, woutb_ref, fnorm_ref, out_ref):
    rc = OUTPUT_ROW_CHUNK
    n_chunks = x1_ref.shape[0] // rc
    rows = lambda c: slice(c * rc, (c + 1) * rc)
    hb, g, y = {}, {}, {}

    def norm_in(c):
        x1 = x1_ref[rows(c), :]
        hb[c] = (x1 * _rms_scale(x1) * bnorm_ref[...]).astype(BF16)

    def gate_proj(c):
        g[c] = _dot(hb.pop(c), wg_ref[...])

    def gate(c):
        gc = g.pop(c)
        y[c] = (o_ref[rows(c), :].astype(F32) * (gc * jax.nn.sigmoid(gc))).astype(BF16)

    def out_proj_norm(c):
        x2 = x1_ref[rows(c), :] + _dot(y.pop(c), woutb_ref[...])
        out_ref[rows(c), :] = x2 * _rms_scale(x2) * fnorm_ref[...]

    stages = (norm_in, gate_proj, gate, out_proj_norm)
    for t in range(n_chunks + len(stages) - 1):
        for depth, stage in reversed(list(enumerate(stages))):
            if 0 <= t - depth < n_chunks:
                stage(t - depth)


def _const_spec(shape):
    return pl.BlockSpec(shape, lambda *_: (0,) * len(shape))


def _rope_tables(positions):
    inv_freq = ROPE_THETA ** (-jnp.arange(0, QK_ROPE, 2, dtype=F32) / QK_ROPE)
    ang = positions.astype(F32)[:, None, :] * inv_freq[None, :, None]
    cos, sin = jnp.cos(ang), jnp.sin(ang)
    return jnp.concatenate([cos, cos], axis=1), jnp.concatenate([-sin, sin], axis=1)


def _swap_halves(w):
    half = w.shape[-1] // 2
    return jnp.concatenate([w[..., half:], w[..., :half]], axis=-1)


def _prepare_weights(a_w_in, w_dkv, w_ukv, b_w_in, b_w_uq):
    win = jnp.concatenate(
        [a_w_in[:, part * E_A + j * CONV_CHUNK:part * E_A + (j + 1) * CONV_CHUNK]
         for j in range(N_CONV_CHUNKS) for part in range(4)], axis=-1).astype(BF16)

    kr = w_dkv[:, KV_RANK:]
    wdkv = jnp.concatenate([w_dkv[:, :KV_RANK], kr, kr, _swap_halves(kr), _swap_halves(kr)],
                           axis=-1).astype(BF16)

    wk = w_ukv.reshape(KV_RANK, N_HEADS, QK_NOPE + V_HEAD)
    wkn = wk[..., :QK_NOPE].reshape(KV_RANK, N_HEADS * QK_NOPE).astype(BF16)
    pad_v = jnp.zeros((KV_RANK, N_HEADS, V_PACK - V_HEAD), F32)
    wvt = jnp.concatenate([wk[..., QK_NOPE:], pad_v], axis=-1).reshape(KV_RANK, N_HEADS * V_PACK)
    wvt = wvt.T.astype(BF16)

    wcq = b_w_in[:, :Q_RANK].astype(BF16)
    wg = b_w_in[:, Q_RANK:].astype(BF16)

    wq = b_w_uq.reshape(Q_RANK, N_HEADS, QK_NOPE + QK_ROPE)
    rope = wq[..., QK_NOPE:]
    wq = jnp.concatenate([wq[..., :QK_NOPE], rope, _swap_halves(rope)], axis=-1)
    wq = wq.reshape(Q_RANK, N_HEADS * HEAD_PACK).astype(BF16)
    return win, wdkv, wkn, wvt, wcq, wg, wq


def kernel(x, positions, a_norm, a_w_in, a_conv, a_w_out, kv_norm, w_dkv, ckv_norm, w_ukv,
           b_norm, b_w_in, b_q_norm, b_w_uq, b_w_out, final_norm):
    bsz, seq, d = x.shape
    assert d == D_MODEL and seq % (TOKEN_SUBTILES * TOKEN_TILE) == 0
    assert seq % ATT_BLOCK == 0 and seq % OUTPUT_TILE == 0
    assert a_norm.shape[0] == 1 and b_norm.shape[0] == 1, "depth-2 trunk: one conv layer, one MLA layer"

    cc, ss = _rope_tables(positions)
    win, wdkv, wkn, wvt, wcq, wg, wq = _prepare_weights(a_w_in[0], w_dkv, w_ukv, b_w_in[0], b_w_uq[0])
    wouta = a_w_out[0].astype(BF16)
    woutb = b_w_out[0].astype(BF16)
    row = lambda g: g.reshape(1, -1).astype(F32)

    tm = TOKEN_TILE
    assert tm == ATT_BLOCK, "v'^T is written one attention kv block per token sub-tile"
    n_tiles = seq // tm
    step_rows = TOKEN_SUBTILES * tm
    hp = N_HEADS * HEAD_PACK
    tok_spec = lambda width: pl.BlockSpec((None, step_rows, width), lambda b, s: (b, s, 0))
    rope_spec = pl.BlockSpec((None, QK_ROPE, step_rows), lambda b, s: (b, 0, s))

    x1, q, k, vt = pl.pallas_call(
        _token_kernel,
        grid=(bsz, seq // step_rows),
        in_specs=[
            tok_spec(D_MODEL), rope_spec, rope_spec,
            _const_spec((1, D_MODEL)), _const_spec(win.shape), _const_spec(a_conv.shape[1:]),
            _const_spec(wouta.shape), _const_spec((1, D_MODEL)), _const_spec(wdkv.shape),
            _const_spec((1, KV_RANK)), _const_spec(wkn.shape), _const_spec(wvt.shape),
            _const_spec((1, D_MODEL)),
            _const_spec(wcq.shape), _const_spec((1, Q_RANK)), _const_spec(wq.shape),
        ],
        out_specs=[tok_spec(D_MODEL), tok_spec(hp), tok_spec(hp),
                   pl.BlockSpec((None, TOKEN_SUBTILES, N_HEADS * V_PACK, tm),
                                lambda b, s: (b, s, 0, 0))],
        out_shape=[
            jax.ShapeDtypeStruct((bsz, seq, D_MODEL), F32),
            jax.ShapeDtypeStruct((bsz, seq, hp), BF16),
            jax.ShapeDtypeStruct((bsz, seq, hp), BF16),
            jax.ShapeDtypeStruct((bsz, n_tiles, N_HEADS * V_PACK, tm), BF16),
        ],
        scratch_shapes=[
            pltpu.VMEM((CARRY_ROWS, E_A), F32),
            pltpu.VMEM((step_rows, E_A), BF16),
        ],
        compiler_params=pltpu.CompilerParams(
            dimension_semantics=("arbitrary", "arbitrary"),
            vmem_limit_bytes=VMEM_LIMIT_BYTES),
        name="token_kernel",
    )(x, cc, ss, row(a_norm[0]), win, a_conv[0], wouta, row(kv_norm), wdkv, row(ckv_norm),
      wkn, wvt, row(b_norm[0]), wcq, row(b_q_norm[0]), wq)

    hw = HEADS_PER_STEP * HEAD_PACK
    att_spec = pl.BlockSpec((None, seq, hw), lambda b, h: (b, 0, h))
    o = pl.pallas_call(
        _attention_kernel,
        grid=(bsz, N_HEADS // HEADS_PER_STEP),
        in_specs=[att_spec, att_spec,
                  pl.BlockSpec((None, n_tiles, HEADS_PER_STEP * V_PACK, ATT_BLOCK),
                               lambda b, h: (b, 0, h, 0))],
        out_specs=pl.BlockSpec((None, seq, HEADS_PER_STEP * V_HEAD), lambda b, h: (b, 0, h)),
        out_shape=jax.ShapeDtypeStruct((bsz, seq, E_B), BF16),
        scratch_shapes=[
            pltpu.VMEM((HEADS_PER_STEP, 1, ATT_BLOCK), F32),
            pltpu.VMEM((HEADS_PER_STEP, 1, ATT_BLOCK), F32),
            pltpu.VMEM((HEADS_PER_STEP, V_PACK, ATT_BLOCK), F32),
        ] + [pltpu.VMEM((ATT_BLOCK, ATT_QCHUNK), F32)
             for _ in range(HEADS_PER_STEP * (ATT_BLOCK // ATT_QCHUNK))],
        compiler_params=pltpu.CompilerParams(
            dimension_semantics=("arbitrary", "arbitrary"),
            vmem_limit_bytes=VMEM_LIMIT_BYTES),
        name="attention_kernel",
    )(q, k, vt)

    out_spec = lambda width: pl.BlockSpec((None, OUTPUT_TILE, width), lambda b, s: (b, s, 0))
    out = pl.pallas_call(
        _output_kernel,
        grid=(bsz, seq // OUTPUT_TILE),
        in_specs=[
            out_spec(D_MODEL), out_spec(E_B), _const_spec((1, D_MODEL)), _const_spec(wg.shape),
            _const_spec(woutb.shape), _const_spec((1, D_MODEL)),
        ],
        out_specs=out_spec(D_MODEL),
        out_shape=jax.ShapeDtypeStruct((bsz, seq, D_MODEL), F32),
        compiler_params=pltpu.CompilerParams(
            dimension_semantics=("arbitrary", "arbitrary"),
            vmem_limit_bytes=VMEM_LIMIT_BYTES),
        name="output_kernel",
    )(x1, o, row(b_norm[0]), wg, woutb, row(final_norm))
    return out
```

```python
import math
from functools import partial

import jax
import jax.numpy as jnp
from jax import lax
from jax.experimental import pallas as pl
from jax.experimental.pallas import tpu as pltpu

D_MODEL = 1024
E_A = D_MODEL
CONV_WIDTH = 3
N_HEADS = 8
QK_NOPE = 64
QK_ROPE = 32
V_HEAD = 64
KV_RANK = 256
Q_RANK = 384
E_B = N_HEADS * V_HEAD
ROPE_THETA = 10000.0
SOFTMAX_SCALE = 1.0 / math.sqrt(QK_NOPE + QK_ROPE)
EPS = 1e-6

LANES = 128
HEAD_PACK = LANES
BF16_SUBLANES = 16
V_PACK = V_HEAD + BF16_SUBLANES
TOKEN_TILE = 512
TOKEN_SUBTILES = 2
OUTPUT_TILE = 2048
OUTPUT_ROW_CHUNK = 256
CONV_CHUNK = 128
N_CONV_CHUNKS = E_A // CONV_CHUNK
CARRY_ROWS = 8
ATT_BLOCK = 512
ATT_QCHUNK = 256
HEADS_PER_STEP = 4
ATT_KV_UNROLL = 2
QK_LOOKAHEAD = 2
MASK_VALUE = -1e30
VMEM_LIMIT_BYTES = 56 * 1024 * 1024

BF16 = jnp.bfloat16
F32 = jnp.float32


def _rms_scale(x):
    return lax.rsqrt(jnp.mean(x * x, axis=-1, keepdims=True) + EPS)


def _dot(a, b):
    return jnp.dot(a, b, preferred_element_type=F32)


_NT_DIMS = (((1,), (1,)), ((), ()))


def _token_kernel(x_ref, cc_ref, ss_ref, win_ref, conv_ref, wouta_ref,
                  wdkv_ref, wkn_ref, wvt_ref, wcq_ref, wq_ref,
                  x1_ref, q_ref, k_ref, vt_ref,
                  carry_ref, y_ref):
    tm = TOKEN_TILE

    @pl.when(pl.program_id(1) == 0)
    def _():
        carry_ref[...] = jnp.zeros((CARRY_ROWS, E_A), F32)

    def normed_input(sub):
        x = x_ref[sub * tm:(sub + 1) * tm, :]
        return x, (x * _rms_scale(x)).astype(BF16)

    x_h = normed_input(0)
    for sub in range(TOKEN_SUBTILES):
        prefetch = partial(normed_input, sub + 1) if sub + 1 < TOKEN_SUBTILES else (lambda: None)
        x_h = _token_subtile(sub * tm, sub, *x_h, prefetch, cc_ref, ss_ref, win_ref, conv_ref,
                             wouta_ref, wdkv_ref, wkn_ref, wvt_ref, wcq_ref, wq_ref,
                             x1_ref, q_ref, k_ref, vt_ref,
                             carry_ref, y_ref)


def _token_subtile(r0, sub, x, h, prefetch_next, cc_ref, ss_ref, win_ref, conv_ref, wouta_ref,
                   wdkv_ref, wkn_ref, wvt_ref, wcq_ref, wq_ref,
                   x1_ref, q_ref, k_ref, vt_ref, carry_ref, y_ref):
    tm = TOKEN_TILE
    cw = CONV_CHUNK
    rows = slice(r0, r0 + tm)

    in_proj = lambda j: _dot(h, win_ref[:, j * 4 * cw:(j + 1) * 4 * cw])
    pj_next = in_proj(0)
    for j in range(N_CONV_CHUNKS):
        cols = slice(j * cw, (j + 1) * cw)
        pj = pj_next
        if j + 1 < N_CONV_CHUNKS:
            pj_next = in_proj(j + 1)
        b = pj[:, 0:cw]
        c = pj[:, cw:2 * cw]
        u = pj[:, 2 * cw:3 * cw]
        g = pj[:, 3 * cw:4 * cw]
        v = c * u
        vv = jnp.concatenate([carry_ref[:, cols], v], axis=0)
        vm1 = pltpu.roll(vv, 1, axis=0)[CARRY_ROWS:]
        vm2 = pltpu.roll(vv, 2, axis=0)[CARRY_ROWS:]
        carry_ref[:, cols] = v[tm - CARRY_ROWS:tm]
        conv = conv_ref[0:1, cols] * vm2 + conv_ref[1:2, cols] * vm1 + conv_ref[2:3, cols] * v
        y = (g * jax.nn.sigmoid(g)) * b * conv
        y_ref[rows, cols] = y.astype(BF16)

    x1 = x + _dot(y_ref[rows, :], wouta_ref[...])
    x1_ref[rows, :] = x1

    xn = (x1 * _rms_scale(x1)).astype(BF16)

    reps = LANES // QK_ROPE
    cc = jnp.concatenate([cc_ref[:, rows]] * reps, axis=0).T
    ss = jnp.concatenate([ss_ref[:, rows]] * reps, axis=0).T

    lane = lax.broadcasted_iota(jnp.int32, (1, LANES), 1)
    low_half = lane < LANES // 2

    ckr = _dot(xn, wdkv_ref[...])
    ckv = ckr[:, 0:KV_RANK]
    rope_terms = ckr[:, KV_RANK:KV_RANK + LANES] * jnp.where(low_half, cc, ss)
    rope_k = rope_terms + pltpu.roll(rope_terms, LANES // 2, axis=1)
    ckvn = (ckv * _rms_scale(ckv)).astype(BF16)
    kn = _dot(ckvn, wkn_ref[...])
    vt = lax.dot_general(wvt_ref[...], ckvn, _NT_DIMS, preferred_element_type=F32)
    head_row = lax.broadcasted_iota(jnp.int32, (N_HEADS * V_PACK, 1), 0) % V_PACK
    vt_ref[sub] = (vt + jnp.where(head_row == V_HEAD, 1.0, 0.0)).astype(BF16)

    next_x_h = prefetch_next()
    cq = _dot(xn, wcq_ref[...])
    cqn = (cq * _rms_scale(cq)).astype(BF16)
    q = _dot(cqn, wq_ref[...])
    qscale = SOFTMAX_SCALE * math.log2(math.e)
    tq = jnp.where(lane < QK_NOPE, 1.0, jnp.where(lane < QK_NOPE + QK_ROPE, cc, ss)) * qscale

    for hd in range(N_HEADS):
        hc = slice(hd * HEAD_PACK, (hd + 1) * HEAD_PACK)
        kn_pair = kn[:, (hd // 2) * LANES:(hd // 2 + 1) * LANES]
        if hd % 2 == 1:
            kn_pair = pltpu.roll(kn_pair, LANES // 2, axis=1)
        k_ref[rows, hc] = jnp.where(low_half, kn_pair, rope_k).astype(BF16)
        q_ref[rows, hc] = (q[:, hc] * tq).astype(BF16)
    return next_x_h


def _attention_kernel(q_ref, k_ref, vt_ref, o_ref, m_ref, alpha_ref, acc_ref, *st_refs):
    blk = ATT_BLOCK
    cw = ATT_QCHUNK
    nblk = q_ref.shape[0] // blk
    chains = [(hd, c) for hd in range(HEADS_PER_STEP) for c in range(blk // cw)]
    assert len(st_refs) == len(chains)

    def head_cols(i):
        hd = chains[i][0]
        return slice(hd * HEAD_PACK, (hd + 1) * HEAD_PACK)

    def state(ref, i):
        hd, c = chains[i]
        return ref.at[hd, :, c * cw:(c + 1) * cw]

    def qk(i, qi, ki, diagonal):
        c = chains[i][1]
        rows = (c + 1) * cw if diagonal else blk
        q = q_ref[pl.ds(pl.multiple_of(qi * blk + c * cw, cw), cw), head_cols(i)]
        k = k_ref[pl.ds(pl.multiple_of(ki * blk, blk), rows), head_cols(i)]
        return lax.dot_general(k, q, _NT_DIMS, preferred_element_type=F32)

    def stage_scores(i, s, diagonal):
        st = st_refs[i]
        if diagonal:
            c = chains[i][1]
            lo, hi = c * cw, (c + 1) * cw
            kv_idx = lax.broadcasted_iota(jnp.int32, (cw, cw), 0)
            q_idx = lax.broadcasted_iota(jnp.int32, (cw, cw), 1)
            tri = jnp.where(kv_idx <= q_idx, s[lo:hi], MASK_VALUE)
            m_new = jnp.max(tri, axis=0, keepdims=True)
            st[lo:hi, :] = tri
            if lo > 0:
                m_new = jnp.maximum(m_new, jnp.max(s[0:lo], axis=0, keepdims=True))
                st[0:lo, :] = s[0:lo]
            if hi < blk:
                st[hi:blk, :] = jnp.full((blk - hi, cw), MASK_VALUE, F32)
            m_old = jnp.full((1, cw), MASK_VALUE, F32)
        else:
            m_old = state(m_ref, i)[...]
            m_new = jnp.maximum(m_old, jnp.max(s, axis=0, keepdims=True))
            st[...] = s
        state(alpha_ref, i)[...] = jnp.exp2(m_old - m_new)
        state(m_ref, i)[...] = m_new

    def accumulate(i, ki, staged_diagonal=False):
        hd, c = chains[i]
        rows = (c + 1) * cw if staged_diagonal else blk
        vt = vt_ref[ki, hd * V_PACK:(hd + 1) * V_PACK, 0:rows]
        p = jnp.exp2(st_refs[i][0:rows, :] - state(m_ref, i)[...]).astype(BF16)
        acc = state(acc_ref, i)
        acc[...] = state(alpha_ref, i)[...] * acc[...] + _dot(vt, p)

    def step(qi, ki, diagonal, staged_ki, staged_diagonal=False):
        s = {j: qk(j, qi, ki, diagonal) for j in range(QK_LOOKAHEAD)}
        for i in range(len(chains)):
            if i + QK_LOOKAHEAD < len(chains):
                s[i + QK_LOOKAHEAD] = qk(i + QK_LOOKAHEAD, qi, ki, diagonal)
            accumulate(i, staged_ki, staged_diagonal)
            stage_scores(i, s.pop(i), diagonal)

    def finalize(qi):
        o_t = jnp.concatenate([acc_ref[hd, 0:V_HEAD, :] / acc_ref[hd, V_HEAD:V_HEAD + 1, :]
                               for hd in range(HEADS_PER_STEP)], axis=0)
        o_ref[pl.ds(pl.multiple_of(qi * blk, blk), blk), :] = o_t.T.astype(o_ref.dtype)

    acc_ref[...] = jnp.zeros(acc_ref.shape, F32)
    for i in range(len(chains)):
        stage_scores(i, qk(i, 0, 0, True), True)

    def q_block(qi, staged_ki):
        step(qi, qi, True, staged_ki)
        finalize(qi - 1)
        step(qi, 0, False, qi, staged_diagonal=True)

        rest = qi - 1
        unroll = ATT_KV_UNROLL

        def kv_group(j, staged):
            for u in range(unroll):
                ki = unroll * j + u + 1
                step(qi, ki, False, staged)
                staged = ki
            return staged

        n_groups = lax.shift_right_logical(rest, unroll.bit_length() - 1)
        staged = lax.fori_loop(0, n_groups, kv_group, 0)

        def kv_single(ki, staged):
            step(qi, ki, False, staged)
            return ki

        return lax.fori_loop(n_groups * unroll + 1, qi, kv_single, staged)

    staged_ki = lax.fori_loop(1, nblk, q_block, 0)
    for i in range(len(chains)):
        accumulate(i, staged_ki)
    finalize(nblk - 1)


def _output_kernel(x1_ref, o_ref, wg_ref, woutb_ref, fnorm_ref, out_ref):
    rc = OUTPUT_ROW_CHUNK
    n_chunks = x1_ref.shape[0] // rc
    rows = lambda c: slice(c * rc, (c + 1) * rc)
    hb, g, y = {}, {}, {}

    def norm_in(c):
        x1 = x1_ref[rows(c), :]
        hb[c] = (x1 * _rms_scale(x1)).astype(BF16)

    def gate_proj(c):
        g[c] = _dot(hb.pop(c), wg_ref[...])

    def gate(c):
        gc = g.pop(c)
        y[c] = (o_ref[rows(c), :].astype(F32) * (gc * jax.nn.sigmoid(gc))).astype(BF16)

    def out_proj_norm(c):
        x2 = x1_ref[rows(c), :] + _dot(y.pop(c), woutb_ref[...])
        out_ref[rows(c), :] = x2 * _rms_scale(x2) * fnorm_ref[...]

    stages = (norm_in, gate_proj, gate, out_proj_norm)
    for t in range(n_chunks + len(stages) - 1):
        for depth, stage in reversed(list(enumerate(stages))):
            if 0 <= t - depth < n_chunks:
                stage(t - depth)


def _const_spec(shape):
    return pl.BlockSpec(shape, lambda *_: (0,) * len(shape))


def _rope_tables(positions):
    inv_freq = ROPE_THETA ** (-jnp.arange(0, QK_ROPE, 2, dtype=F32) / QK_ROPE)
    ang = positions.astype(F32)[:, None, :] * inv_freq[None, :, None]
    cos, sin = jnp.cos(ang), jnp.sin(ang)
    return jnp.concatenate([cos, cos], axis=1), jnp.concatenate([-sin, sin], axis=1)


def _swap_halves(w):
    half = w.shape[-1] // 2
    return jnp.concatenate([w[..., half:], w[..., :half]], axis=-1)


def _prepare_weights(a_norm, a_w_in, kv_norm, w_dkv, ckv_norm, w_ukv, b_norm, b_w_in, q_norm, b_w_uq):
    rows = lambda g: g.astype(F32)[:, None]

    a_w_in = rows(a_norm) * a_w_in
    win = jnp.concatenate(
        [a_w_in[:, part * E_A + j * CONV_CHUNK:part * E_A + (j + 1) * CONV_CHUNK]
         for j in range(N_CONV_CHUNKS) for part in range(4)], axis=-1).astype(BF16)

    w_dkv = rows(kv_norm) * w_dkv
    kr = w_dkv[:, KV_RANK:]
    wdkv = jnp.concatenate([w_dkv[:, :KV_RANK], kr, kr, _swap_halves(kr), _swap_halves(kr)],
                           axis=-1).astype(BF16)

    wk = (rows(ckv_norm) * w_ukv).reshape(KV_RANK, N_HEADS, QK_NOPE + V_HEAD)
    wkn = wk[..., :QK_NOPE].reshape(KV_RANK, N_HEADS * QK_NOPE).astype(BF16)
    pad_v = jnp.zeros((KV_RANK, N_HEADS, V_PACK - V_HEAD), F32)
    wvt = jnp.concatenate([wk[..., QK_NOPE:], pad_v], axis=-1).reshape(KV_RANK, N_HEADS * V_PACK)
    wvt = wvt.T.astype(BF16)

    b_w_in = rows(b_norm) * b_w_in
    wcq = b_w_in[:, :Q_RANK].astype(BF16)
    wg = b_w_in[:, Q_RANK:].astype(BF16)

    wq = (rows(q_norm) * b_w_uq).reshape(Q_RANK, N_HEADS, QK_NOPE + QK_ROPE)
    rope = wq[..., QK_NOPE:]
    wq = jnp.concatenate([wq[..., :QK_NOPE], rope, _swap_halves(rope)], axis=-1)
    wq = wq.reshape(Q_RANK, N_HEADS * HEAD_PACK).astype(BF16)
    return win, wdkv, wkn, wvt, wcq, wg, wq


def kernel(x, positions, a_norm, a_w_in, a_conv, a_w_out, kv_norm, w_dkv, ckv_norm, w_ukv,
           b_norm, b_w_in, b_q_norm, b_w_uq, b_w_out, final_norm):
    bsz, seq, d = x.shape
    assert d == D_MODEL and seq % (TOKEN_SUBTILES * TOKEN_TILE) == 0
    assert seq % ATT_BLOCK == 0 and seq % OUTPUT_TILE == 0
    assert a_norm.shape[0] == 1 and b_norm.shape[0] == 1, "depth-2 trunk: one conv layer, one MLA layer"

    cc, ss = _rope_tables(positions)
    win, wdkv, wkn, wvt, wcq, wg, wq = _prepare_weights(
        a_norm[0], a_w_in[0], kv_norm, w_dkv, ckv_norm, w_ukv, b_norm[0], b_w_in[0], b_q_norm[0], b_w_uq[0])
    wouta = a_w_out[0].astype(BF16)
    woutb = b_w_out[0].astype(BF16)

    tm = TOKEN_TILE
    assert tm == ATT_BLOCK, "v'^T is written one attention kv block per token sub-tile"
    n_tiles = seq // tm
    step_rows = TOKEN_SUBTILES * tm
    hp = N_HEADS * HEAD_PACK
    tok_spec = lambda width: pl.BlockSpec((None, step_rows, width), lambda b, s: (b, s, 0))
    rope_spec = pl.BlockSpec((None, QK_ROPE, step_rows), lambda b, s: (b, 0, s))

    x1, q, k, vt = pl.pallas_call(
        _token_kernel,
        grid=(bsz, seq // step_rows),
        in_specs=[
            tok_spec(D_MODEL), rope_spec, rope_spec,
            _const_spec(win.shape), _const_spec(a_conv.shape[1:]), _const_spec(wouta.shape),
            _const_spec(wdkv.shape), _const_spec(wkn.shape), _const_spec(wvt.shape),
            _const_spec(wcq.shape), _const_spec(wq.shape),
        ],
        out_specs=[tok_spec(D_MODEL), tok_spec(hp), tok_spec(hp),
                   pl.BlockSpec((None, TOKEN_SUBTILES, N_HEADS * V_PACK, tm),
                                lambda b, s: (b, s, 0, 0))],
        out_shape=[
            jax.ShapeDtypeStruct((bsz, seq, D_MODEL), F32),
            jax.ShapeDtypeStruct((bsz, seq, hp), BF16),
            jax.ShapeDtypeStruct((bsz, seq, hp), BF16),
            jax.ShapeDtypeStruct((bsz, n_tiles, N_HEADS * V_PACK, tm), BF16),
        ],
        scratch_shapes=[
            pltpu.VMEM((CARRY_ROWS, E_A), F32),
            pltpu.VMEM((step_rows, E_A), BF16),
        ],
        compiler_params=pltpu.CompilerParams(
            dimension_semantics=("arbitrary", "arbitrary"),
            vmem_limit_bytes=VMEM_LIMIT_BYTES),
        name="token_kernel",
    )(x, cc, ss, win, a_conv[0], wouta, wdkv, wkn, wvt, wcq, wq)

    hw = HEADS_PER_STEP * HEAD_PACK
    att_spec = pl.BlockSpec((None, seq, hw), lambda b, h: (b, 0, h))
    o = pl.pallas_call(
        _attention_kernel,
        grid=(bsz, N_HEADS // HEADS_PER_STEP),
        in_specs=[att_spec, att_spec,
                  pl.BlockSpec((None, n_tiles, HEADS_PER_STEP * V_PACK, ATT_BLOCK),
                               lambda b, h: (b, 0, h, 0))],
        out_specs=pl.BlockSpec((None, seq, HEADS_PER_STEP * V_HEAD), lambda b, h: (b, 0, h)),
        out_shape=jax.ShapeDtypeStruct((bsz, seq, E_B), BF16),
        scratch_shapes=[
            pltpu.VMEM((HEADS_PER_STEP, 1, ATT_BLOCK), F32),
            pltpu.VMEM((HEADS_PER_STEP, 1, ATT_BLOCK), F32),
            pltpu.VMEM((HEADS_PER_STEP, V_PACK, ATT_BLOCK), F32),
        ] + [pltpu.VMEM((ATT_BLOCK, ATT_QCHUNK), F32)
             for _ in range(HEADS_PER_STEP * (ATT_BLOCK // ATT_QCHUNK))],
        compiler_params=pltpu.CompilerParams(
            dimension_semantics=("arbitrary", "arbitrary"),
            vmem_limit_bytes=VMEM_LIMIT_BYTES),
        name="attention_kernel",
    )(q, k, vt)

    out_spec = lambda width: pl.BlockSpec((None, OUTPUT_TILE, width), lambda b, s: (b, s, 0))
    out = pl.pallas_call(
        _output_kernel,
        grid=(bsz, seq // OUTPUT_TILE),
        in_specs=[
            out_spec(D_MODEL), out_spec(E_B), _const_spec(wg.shape), _const_spec(woutb.shape),
            _const_spec((1, D_MODEL)),
        ],
        out_specs=out_spec(D_MODEL),
        out_shape=jax.ShapeDtypeStruct((bsz, seq, D_MODEL), F32),
        compiler_params=pltpu.CompilerParams(
            dimension_semantics=("arbitrary", "arbitrary"),
            vmem_limit_bytes=VMEM_LIMIT_BYTES),
        name="output_kernel",
    )(x1, o, wg, woutb, final_norm.reshape(1, D_MODEL).astype(F32))
    return out
```

```python
import math
from functools import partial

import jax
import jax.numpy as jnp
from jax import lax
from jax.experimental import pallas as pl
from jax.experimental.pallas import tpu as pltpu

D_MODEL = 1024
E_A = D_MODEL
CONV_WIDTH = 3
N_HEADS = 8
QK_NOPE = 64
QK_ROPE = 32
V_HEAD = 64
KV_RANK = 256
Q_RANK = 384
E_B = N_HEADS * V_HEAD
ROPE_THETA = 10000.0
SOFTMAX_SCALE = 1.0 / math.sqrt(QK_NOPE + QK_ROPE)
EPS = 1e-6

LANES = 128
SUBLANES = 8
BF16_SUBLANES = 2 * SUBLANES
MXU_TILE = 256
V7X_VMEM_BYTES = 64 * 1024 * 1024
HEAD_PACK = LANES
V_PACK = V_HEAD + BF16_SUBLANES
TOKEN_TILE = 512
TOKEN_SUBTILES = 2
OUTPUT_TILE = 2048
OUTPUT_ROW_CHUNK = 256
CONV_CHUNK = 128
N_CONV_CHUNKS = E_A // CONV_CHUNK
CARRY_ROWS = SUBLANES
ATT_BLOCK = 512
ATT_QCHUNK = MXU_TILE
HEADS_PER_STEP = 4
ATT_KV_UNROLL = 2
QK_LOOKAHEAD = 2
MASK_VALUE = float(jnp.finfo(jnp.float32).min)
VMEM_LIMIT_BYTES = V7X_VMEM_BYTES * 7 // 8

BF16 = jnp.bfloat16
F32 = jnp.float32


def _rms_scale(x):
    return lax.rsqrt(jnp.mean(x * x, axis=-1, keepdims=True) + EPS)


def _dot(a, b):
    return jnp.dot(a, b, preferred_element_type=F32)


_NT_DIMS = (((1,), (1,)), ((), ()))


def _token_kernel(x_ref, cc_ref, ss_ref, win_ref, conv_ref, wouta_ref,
                  wdkv_ref, wkn_ref, wvt_ref, wcq_ref, wq_ref,
                  x1_ref, q_ref, k_ref, vt_ref,
                  carry_ref, y_ref):
    tm = TOKEN_TILE

    @pl.when(pl.program_id(1) == 0)
    def _():
        carry_ref[...] = jnp.zeros((CARRY_ROWS, E_A), F32)

    def normed_input(sub):
        x = x_ref[sub * tm:(sub + 1) * tm, :]
        return x, (x * _rms_scale(x)).astype(BF16)

    x_h = normed_input(0)
    for sub in range(TOKEN_SUBTILES):
        prefetch = partial(normed_input, sub + 1) if sub + 1 < TOKEN_SUBTILES else (lambda: None)
        x_h = _token_subtile(sub * tm, sub, *x_h, prefetch, cc_ref, ss_ref, win_ref, conv_ref,
                             wouta_ref, wdkv_ref, wkn_ref, wvt_ref, wcq_ref, wq_ref,
                             x1_ref, q_ref, k_ref, vt_ref,
                             carry_ref, y_ref)


def _token_subtile(r0, sub, x, h, prefetch_next, cc_ref, ss_ref, win_ref, conv_ref, wouta_ref,
                   wdkv_ref, wkn_ref, wvt_ref, wcq_ref, wq_ref,
                   x1_ref, q_ref, k_ref, vt_ref, carry_ref, y_ref):
    tm = TOKEN_TILE
    cw = CONV_CHUNK
    rows = slice(r0, r0 + tm)

    in_proj = lambda j: _dot(h, win_ref[:, j * 4 * cw:(j + 1) * 4 * cw])
    pj_next = in_proj(0)
    for j in range(N_CONV_CHUNKS):
        cols = slice(j * cw, (j + 1) * cw)
        pj = pj_next
        if j + 1 < N_CONV_CHUNKS:
            pj_next = in_proj(j + 1)
        b = pj[:, 0:cw]
        c = pj[:, cw:2 * cw]
        u = pj[:, 2 * cw:3 * cw]
        g = pj[:, 3 * cw:4 * cw]
        v = c * u
        vv = jnp.concatenate([carry_ref[:, cols], v], axis=0)
        carry_ref[:, cols] = v[tm - CARRY_ROWS:tm]
        conv = None
        for k in range(CONV_WIDTH):
            shift = CONV_WIDTH - 1 - k
            tap = conv_ref[k:k + 1, cols] * (pltpu.roll(vv, shift, axis=0)[CARRY_ROWS:] if shift else v)
            conv = tap if conv is None else conv + tap
        y = (g * jax.nn.sigmoid(g)) * b * conv
        y_ref[rows, cols] = y.astype(BF16)

    x1 = x + _dot(y_ref[rows, :], wouta_ref[...])
    x1_ref[rows, :] = x1

    xn = (x1 * _rms_scale(x1)).astype(BF16)

    reps = LANES // QK_ROPE
    cc = jnp.concatenate([cc_ref[:, rows]] * reps, axis=0).T
    ss = jnp.concatenate([ss_ref[:, rows]] * reps, axis=0).T

    lane = lax.broadcasted_iota(jnp.int32, (1, LANES), 1)
    low_half = lane < LANES // 2

    ckr = _dot(xn, wdkv_ref[...])
    ckv = ckr[:, 0:KV_RANK]
    rope_terms = ckr[:, KV_RANK:KV_RANK + LANES] * jnp.where(low_half, cc, ss)
    rope_k = rope_terms + pltpu.roll(rope_terms, LANES // 2, axis=1)
    ckvn = (ckv * _rms_scale(ckv)).astype(BF16)
    kn = _dot(ckvn, wkn_ref[...])
    vt = lax.dot_general(wvt_ref[...], ckvn, _NT_DIMS, preferred_element_type=F32)
    head_row = lax.broadcasted_iota(jnp.int32, (N_HEADS * V_PACK, 1), 0) % V_PACK
    vt_ref[sub] = (vt + jnp.where(head_row == V_HEAD, 1.0, 0.0)).astype(BF16)

    next_x_h = prefetch_next()
    cq = _dot(xn, wcq_ref[...])
    cqn = (cq * _rms_scale(cq)).astype(BF16)
    q = _dot(cqn, wq_ref[...])
    qscale = SOFTMAX_SCALE * math.log2(math.e)
    tq = jnp.where(lane < QK_NOPE, 1.0, jnp.where(lane < QK_NOPE + QK_ROPE, cc, ss)) * qscale

    for hd in range(N_HEADS):
        hc = slice(hd * HEAD_PACK, (hd + 1) * HEAD_PACK)
        kn_pair = kn[:, (hd // 2) * LANES:(hd // 2 + 1) * LANES]
        if hd % 2 == 1:
            kn_pair = pltpu.roll(kn_pair, LANES // 2, axis=1)
        k_ref[rows, hc] = jnp.where(low_half, kn_pair, rope_k).astype(BF16)
        q_ref[rows, hc] = (q[:, hc] * tq).astype(BF16)
    return next_x_h


def _attention_kernel(q_ref, k_ref, vt_ref, o_ref, m_ref, alpha_ref, acc_ref, *st_refs):
    blk = ATT_BLOCK
    cw = ATT_QCHUNK
    nblk = q_ref.shape[0] // blk
    chains = [(hd, c) for hd in range(HEADS_PER_STEP) for c in range(blk // cw)]
    assert len(st_refs) == len(chains)

    def head_cols(i):
        hd = chains[i][0]
        return slice(hd * HEAD_PACK, (hd + 1) * HEAD_PACK)

    def state(ref, i):
        hd, c = chains[i]
        return ref.at[hd, :, c * cw:(c + 1) * cw]

    def qk(i, qi, ki, diagonal):
        c = chains[i][1]
        rows = (c + 1) * cw if diagonal else blk
        q = q_ref[pl.ds(pl.multiple_of(qi * blk + c * cw, cw), cw), head_cols(i)]
        k = k_ref[pl.ds(pl.multiple_of(ki * blk, blk), rows), head_cols(i)]
        return lax.dot_general(k, q, _NT_DIMS, preferred_element_type=F32)

    def stage_scores(i, s, diagonal):
        st = st_refs[i]
        if diagonal:
            c = chains[i][1]
            lo, hi = c * cw, (c + 1) * cw
            kv_idx = lax.broadcasted_iota(jnp.int32, (cw, cw), 0)
            q_idx = lax.broadcasted_iota(jnp.int32, (cw, cw), 1)
            tri = jnp.where(kv_idx <= q_idx, s[lo:hi], MASK_VALUE)
            m_new = jnp.max(tri, axis=0, keepdims=True)
            st[lo:hi, :] = tri
            if lo > 0:
                m_new = jnp.maximum(m_new, jnp.max(s[0:lo], axis=0, keepdims=True))
                st[0:lo, :] = s[0:lo]
            if hi < blk:
                st[hi:blk, :] = jnp.full((blk - hi, cw), MASK_VALUE, F32)
            m_old = jnp.full((1, cw), MASK_VALUE, F32)
        else:
            m_old = state(m_ref, i)[...]
            m_new = jnp.maximum(m_old, jnp.max(s, axis=0, keepdims=True))
            st[...] = s
        state(alpha_ref, i)[...] = jnp.exp2(m_old - m_new)
        state(m_ref, i)[...] = m_new

    def accumulate(i, ki, staged_diagonal=False):
        hd, c = chains[i]
        rows = (c + 1) * cw if staged_diagonal else blk
        vt = vt_ref[ki, hd * V_PACK:(hd + 1) * V_PACK, 0:rows]
        p = jnp.exp2(st_refs[i][0:rows, :] - state(m_ref, i)[...]).astype(BF16)
        acc = state(acc_ref, i)
        acc[...] = state(alpha_ref, i)[...] * acc[...] + _dot(vt, p)

    def step(qi, ki, diagonal, staged_ki, staged_diagonal=False):
        s = {j: qk(j, qi, ki, diagonal) for j in range(QK_LOOKAHEAD)}
        for i in range(len(chains)):
            if i + QK_LOOKAHEAD < len(chains):
                s[i + QK_LOOKAHEAD] = qk(i + QK_LOOKAHEAD, qi, ki, diagonal)
            accumulate(i, staged_ki, staged_diagonal)
            stage_scores(i, s.pop(i), diagonal)

    def finalize(qi):
        o_t = jnp.concatenate([acc_ref[hd, 0:V_HEAD, :] / acc_ref[hd, V_HEAD:V_HEAD + 1, :]
                               for hd in range(HEADS_PER_STEP)], axis=0)
        o_ref[pl.ds(pl.multiple_of(qi * blk, blk), blk), :] = o_t.T.astype(o_ref.dtype)

    acc_ref[...] = jnp.zeros(acc_ref.shape, F32)
    for i in range(len(chains)):
        stage_scores(i, qk(i, 0, 0, True), True)

    def q_block(qi, staged_ki):
        step(qi, qi, True, staged_ki)
        finalize(qi - 1)
        step(qi, 0, False, qi, staged_diagonal=True)

        rest = qi - 1
        unroll = ATT_KV_UNROLL

        def kv_group(j, staged):
            for u in range(unroll):
                ki = unroll * j + u + 1
                step(qi, ki, False, staged)
                staged = ki
            return staged

        n_groups = lax.shift_right_logical(rest, unroll.bit_length() - 1)
        staged = lax.fori_loop(0, n_groups, kv_group, 0)

        def kv_single(ki, staged):
            step(qi, ki, False, staged)
            return ki

        return lax.fori_loop(n_groups * unroll + 1, qi, kv_single, staged)

    staged_ki = lax.fori_loop(1, nblk, q_block, 0)
    for i in range(len(chains)):
        accumulate(i, staged_ki)
    finalize(nblk - 1)


def _output_kernel(x1_ref, o_ref, wg_ref, woutb_ref, fnorm_ref, out_ref):
    rc = OUTPUT_ROW_CHUNK
    n_chunks = x1_ref.shape[0] // rc
    rows = lambda c: slice(c * rc, (c + 1) * rc)
    hb, g, y = {}, {}, {}

    def norm_in(c):
        x1 = x1_ref[rows(c), :]
        hb[c] = (x1 * _rms_scale(x1)).astype(BF16)

    def gate_proj(c):
        g[c] = _dot(hb.pop(c), wg_ref[...])

    def gate(c):
        gc = g.pop(c)
        y[c] = (o_ref[rows(c), :].astype(F32) * (gc * jax.nn.sigmoid(gc))).astype(BF16)

    def out_proj_norm(c):
        x2 = x1_ref[rows(c), :] + _dot(y.pop(c), woutb_ref[...])
        out_ref[rows(c), :] = x2 * _rms_scale(x2) * fnorm_ref[...]

    stages = (norm_in, gate_proj, gate, out_proj_norm)
    for t in range(n_chunks + len(stages) - 1):
        for depth, stage in reversed(list(enumerate(stages))):
            if 0 <= t - depth < n_chunks:
                stage(t - depth)


def _const_spec(shape):
    return pl.BlockSpec(shape, lambda *_: (0,) * len(shape))


def _rope_tables(positions):
    inv_freq = ROPE_THETA ** (-jnp.arange(0, QK_ROPE, 2, dtype=F32) / QK_ROPE)
    ang = positions.astype(F32)[:, None, :] * inv_freq[None, :, None]
    cos, sin = jnp.cos(ang), jnp.sin(ang)
    return jnp.concatenate([cos, cos], axis=1), jnp.concatenate([-sin, sin], axis=1)


def _swap_halves(w):
    half = w.shape[-1] // 2
    return jnp.concatenate([w[..., half:], w[..., :half]], axis=-1)


def _prepare_weights(a_norm, a_w_in, kv_norm, w_dkv, ckv_norm, w_ukv, b_norm, b_w_in, q_norm, b_w_uq):
    rows = lambda g: g.astype(F32)[:, None]

    a_w_in = rows(a_norm) * a_w_in
    win = jnp.concatenate(
        [a_w_in[:, part * E_A + j * CONV_CHUNK:part * E_A + (j + 1) * CONV_CHUNK]
         for j in range(N_CONV_CHUNKS) for part in range(4)], axis=-1).astype(BF16)

    w_dkv = rows(kv_norm) * w_dkv
    kr = w_dkv[:, KV_RANK:]
    wdkv = jnp.concatenate([w_dkv[:, :KV_RANK], kr, kr, _swap_halves(kr), _swap_halves(kr)],
                           axis=-1).astype(BF16)

    wk = (rows(ckv_norm) * w_ukv).reshape(KV_RANK, N_HEADS, QK_NOPE + V_HEAD)
    wkn = wk[..., :QK_NOPE].reshape(KV_RANK, N_HEADS * QK_NOPE).astype(BF16)
    pad_v = jnp.zeros((KV_RANK, N_HEADS, V_PACK - V_HEAD), F32)
    wvt = jnp.concatenate([wk[..., QK_NOPE:], pad_v], axis=-1).reshape(KV_RANK, N_HEADS * V_PACK)
    wvt = wvt.T.astype(BF16)

    b_w_in = rows(b_norm) * b_w_in
    wcq = b_w_in[:, :Q_RANK].astype(BF16)
    wg = b_w_in[:, Q_RANK:].astype(BF16)

    wq = (rows(q_norm) * b_w_uq).reshape(Q_RANK, N_HEADS, QK_NOPE + QK_ROPE)
    rope = wq[..., QK_NOPE:]
    wq = jnp.concatenate([wq[..., :QK_NOPE], rope, _swap_halves(rope)], axis=-1)
    wq = wq.reshape(Q_RANK, N_HEADS * HEAD_PACK).astype(BF16)
    return win, wdkv, wkn, wvt, wcq, wg, wq


def kernel(x, positions, a_norm, a_w_in, a_conv, a_w_out, kv_norm, w_dkv, ckv_norm, w_ukv,
           b_norm, b_w_in, b_q_norm, b_w_uq, b_w_out, final_norm):
    bsz, seq, d = x.shape
    assert d == D_MODEL and seq % (TOKEN_SUBTILES * TOKEN_TILE) == 0
    assert seq % ATT_BLOCK == 0 and seq % OUTPUT_TILE == 0
    assert a_norm.shape[0] == 1 and b_norm.shape[0] == 1, "depth-2 trunk: one conv layer, one MLA layer"
    assert a_conv.shape[1:] == (CONV_WIDTH, E_A) and CONV_WIDTH - 1 <= CARRY_ROWS

    cc, ss = _rope_tables(positions)
    win, wdkv, wkn, wvt, wcq, wg, wq = _prepare_weights(
        a_norm[0], a_w_in[0], kv_norm, w_dkv, ckv_norm, w_ukv, b_norm[0], b_w_in[0], b_q_norm[0], b_w_uq[0])
    wouta = a_w_out[0].astype(BF16)
    woutb = b_w_out[0].astype(BF16)

    tm = TOKEN_TILE
    assert tm == ATT_BLOCK, "v'^T is written one attention kv block per token sub-tile"
    n_tiles = seq // tm
    step_rows = TOKEN_SUBTILES * tm
    hp = N_HEADS * HEAD_PACK
    tok_spec = lambda width: pl.BlockSpec((None, step_rows, width), lambda b, s: (b, s, 0))
    rope_spec = pl.BlockSpec((None, QK_ROPE, step_rows), lambda b, s: (b, 0, s))

    x1, q, k, vt = pl.pallas_call(
        _token_kernel,
        grid=(bsz, seq // step_rows),
        in_specs=[
            tok_spec(D_MODEL), rope_spec, rope_spec,
            _const_spec(win.shape), _const_spec(a_conv.shape[1:]), _const_spec(wouta.shape),
            _const_spec(wdkv.shape), _const_spec(wkn.shape), _const_spec(wvt.shape),
            _const_spec(wcq.shape), _const_spec(wq.shape),
        ],
        out_specs=[tok_spec(D_MODEL), tok_spec(hp), tok_spec(hp),
                   pl.BlockSpec((None, TOKEN_SUBTILES, N_HEADS * V_PACK, tm),
                                lambda b, s: (b, s, 0, 0))],
        out_shape=[
            jax.ShapeDtypeStruct((bsz, seq, D_MODEL), F32),
            jax.ShapeDtypeStruct((bsz, seq, hp), BF16),
            jax.ShapeDtypeStruct((bsz, seq, hp), BF16),
            jax.ShapeDtypeStruct((bsz, n_tiles, N_HEADS * V_PACK, tm), BF16),
        ],
        scratch_shapes=[
            pltpu.VMEM((CARRY_ROWS, E_A), F32),
            pltpu.VMEM((step_rows, E_A), BF16),
        ],
        compiler_params=pltpu.CompilerParams(
            dimension_semantics=("arbitrary", "arbitrary"),
            vmem_limit_bytes=VMEM_LIMIT_BYTES),
        name="token_kernel",
    )(x, cc, ss, win, a_conv[0], wouta, wdkv, wkn, wvt, wcq, wq)

    hw = HEADS_PER_STEP * HEAD_PACK
    att_spec = pl.BlockSpec((None, seq, hw), lambda b, h: (b, 0, h))
    o = pl.pallas_call(
        _attention_kernel,
        grid=(bsz, N_HEADS // HEADS_PER_STEP),
        in_specs=[att_spec, att_spec,
                  pl.BlockSpec((None, n_tiles, HEADS_PER_STEP * V_PACK, ATT_BLOCK),
                               lambda b, h: (b, 0, h, 0))],
        out_specs=pl.BlockSpec((None, seq, HEADS_PER_STEP * V_HEAD), lambda b, h: (b, 0, h)),
        out_shape=jax.ShapeDtypeStruct((bsz, seq, E_B), BF16),
        scratch_shapes=[
            pltpu.VMEM((HEADS_PER_STEP, 1, ATT_BLOCK), F32),
            pltpu.VMEM((HEADS_PER_STEP, 1, ATT_BLOCK), F32),
            pltpu.VMEM((HEADS_PER_STEP, V_PACK, ATT_BLOCK), F32),
        ] + [pltpu.VMEM((ATT_BLOCK, ATT_QCHUNK), F32)
             for _ in range(HEADS_PER_STEP * (ATT_BLOCK // ATT_QCHUNK))],
        compiler_params=pltpu.CompilerParams(
            dimension_semantics=("arbitrary", "arbitrary"),
            vmem_limit_bytes=VMEM_LIMIT_BYTES),
        name="attention_kernel",
    )(q, k, vt)

    out_spec = lambda width: pl.BlockSpec((None, OUTPUT_TILE, width), lambda b, s: (b, s, 0))
    out = pl.pallas_call(
        _output_kernel,
        grid=(bsz, seq // OUTPUT_TILE),
        in_specs=[
            out_spec(D_MODEL), out_spec(E_B), _const_spec(wg.shape), _const_spec(woutb.shape),
            _const_spec((1, D_MODEL)),
        ],
        out_specs=out_spec(D_MODEL),
        out_shape=jax.ShapeDtypeStruct((bsz, seq, D_MODEL), F32),
        compiler_params=pltpu.CompilerParams(
            dimension_semantics=("arbitrary", "arbitrary"),
            vmem_limit_bytes=VMEM_LIMIT_BYTES),
        name="output_kernel",
    )(x1, o, wg, woutb, final_norm.reshape(1, D_MODEL).astype(F32))
    return out
```

```python
import math

import jax
import jax.numpy as jnp
from jax import lax
from jax.experimental import pallas as pl
from jax.experimental.pallas import tpu as pltpu

D_MODEL = 1024
E_A = D_MODEL
CONV_WIDTH = 3
N_HEADS = 8
QK_NOPE = 64
QK_ROPE = 32
V_HEAD = 64
KV_RANK = 256
Q_RANK = 384
E_B = N_HEADS * V_HEAD
ROPE_THETA = 10000.0
SOFTMAX_SCALE = 1.0 / math.sqrt(QK_NOPE + QK_ROPE)
EPS = 1e-6

LANES = 128
SUBLANES = 8
BF16_SUBLANES = 2 * SUBLANES
MXU_TILE = 256
V7X_VMEM_BYTES = 64 * 1024 * 1024
HEAD_PACK = LANES
V_PACK = V_HEAD + BF16_SUBLANES
TOKEN_TILE = 512
TOKEN_SUBTILES = 2
OUTPUT_TILE = 2048
OUTPUT_ROW_CHUNK = 256
CONV_CHUNK = 128
N_CONV_CHUNKS = E_A // CONV_CHUNK
CARRY_ROWS = SUBLANES
ATT_BLOCK = 512
ATT_QCHUNK = MXU_TILE
HEADS_PER_STEP = 4
ATT_KV_UNROLL = 2
QK_LOOKAHEAD = 2
MASK_VALUE = float(jnp.finfo(jnp.float32).min)
VMEM_LIMIT_BYTES = V7X_VMEM_BYTES * 7 // 8

BF16 = jnp.bfloat16
F32 = jnp.float32


def _rms_scale(x):
    return lax.rsqrt(jnp.mean(x * x, axis=-1, keepdims=True) + EPS)


def _dot(a, b):
    return jnp.dot(a, b, preferred_element_type=F32)


_NT_DIMS = (((1,), (1,)), ((), ()))


def _token_kernel(x_ref, cc_ref, ss_ref, win_ref, conv_ref, wouta_ref,
                  wdkv_ref, wkn_ref, wvt_ref, wcq_ref, wq_ref,
                  x1_ref, q_ref, k_ref, vt_ref,
                  carry_ref, y_ref):
    @pl.when(pl.program_id(1) == 0)
    def _():
        carry_ref[...] = jnp.zeros((CARRY_ROWS, E_A), F32)

    xn = [_conv_mixer_subtile(sub, x_ref, win_ref, conv_ref, wouta_ref, x1_ref, carry_ref, y_ref)
          for sub in range(TOKEN_SUBTILES)]
    _projections(xn, cc_ref, ss_ref, wdkv_ref, wkn_ref, wvt_ref, wcq_ref, wq_ref, q_ref, k_ref, vt_ref)


def _conv_mixer_subtile(sub, x_ref, win_ref, conv_ref, wouta_ref, x1_ref, carry_ref, y_ref):
    tm = TOKEN_TILE
    cw = CONV_CHUNK
    rows = slice(sub * tm, (sub + 1) * tm)
    x = x_ref[rows, :]
    h = (x * _rms_scale(x)).astype(BF16)

    for j in range(N_CONV_CHUNKS):
        cols = slice(j * cw, (j + 1) * cw)
        pj = _dot(h, win_ref[:, j * 4 * cw:(j + 1) * 4 * cw])
        b = pj[:, 0:cw]
        c = pj[:, cw:2 * cw]
        u = pj[:, 2 * cw:3 * cw]
        g = pj[:, 3 * cw:4 * cw]
        v = c * u
        vv = jnp.concatenate([carry_ref[:, cols], v], axis=0)
        carry_ref[:, cols] = v[tm - CARRY_ROWS:tm]
        conv = None
        for k in range(CONV_WIDTH):
            shift = CONV_WIDTH - 1 - k
            tap = conv_ref[k:k + 1, cols] * (pltpu.roll(vv, shift, axis=0)[CARRY_ROWS:] if shift else v)
            conv = tap if conv is None else conv + tap
        y = (g * jax.nn.sigmoid(g)) * b * conv
        y_ref[rows, cols] = y.astype(BF16)

    x1 = x + _dot(y_ref[rows, :], wouta_ref[...])
    x1_ref[rows, :] = x1
    return (x1 * _rms_scale(x1)).astype(BF16)


def _projections(xn, cc_ref, ss_ref, wdkv_ref, wkn_ref, wvt_ref, wcq_ref, wq_ref, q_ref, k_ref, vt_ref):
    tm = TOKEN_TILE
    subs = range(len(xn))
    rows = [slice(s * tm, (s + 1) * tm) for s in subs]
    lane = lax.broadcasted_iota(jnp.int32, (1, LANES), 1)
    low_half = lane < LANES // 2

    reps = LANES // QK_ROPE
    cc = [jnp.concatenate([cc_ref[:, r]] * reps, axis=0).T for r in rows]
    ss = [jnp.concatenate([ss_ref[:, r]] * reps, axis=0).T for r in rows]

    ckr = [_dot(xn[s], wdkv_ref[...]) for s in subs]
    cq = [_dot(xn[s], wcq_ref[...]) for s in subs]
    rope_terms = [ckr[s][:, KV_RANK:KV_RANK + LANES] * jnp.where(low_half, cc[s], ss[s]) for s in subs]
    rope_k = [t + pltpu.roll(t, LANES // 2, axis=1) for t in rope_terms]
    ckvn = [(ckr[s][:, 0:KV_RANK] * _rms_scale(ckr[s][:, 0:KV_RANK])).astype(BF16) for s in subs]
    cqn = [(c * _rms_scale(c)).astype(BF16) for c in cq]
    kn = [_dot(c, wkn_ref[...]) for c in ckvn]
    q = [_dot(c, wq_ref[...]) for c in cqn]
    head_row = lax.broadcasted_iota(jnp.int32, (N_HEADS * V_PACK, 1), 0) % V_PACK
    ones_row = jnp.where(head_row == V_HEAD, 1.0, 0.0)
    for s in subs:
        vt = lax.dot_general(wvt_ref[...], ckvn[s], _NT_DIMS, preferred_element_type=F32)
        vt_ref[s] = (vt + ones_row).astype(BF16)

    qscale = SOFTMAX_SCALE * math.log2(math.e)
    for s in subs:
        tq = jnp.where(lane < QK_NOPE, 1.0, jnp.where(lane < QK_NOPE + QK_ROPE, cc[s], ss[s])) * qscale
        for hd in range(N_HEADS):
            hc = slice(hd * HEAD_PACK, (hd + 1) * HEAD_PACK)
            kn_pair = kn[s][:, (hd // 2) * LANES:(hd // 2 + 1) * LANES]
            if hd % 2 == 1:
                kn_pair = pltpu.roll(kn_pair, LANES // 2, axis=1)
            k_ref[rows[s], hc] = jnp.where(low_half, kn_pair, rope_k[s]).astype(BF16)
            q_ref[rows[s], hc] = (q[s][:, hc] * tq).astype(BF16)


def _attention_kernel(q_ref, k_ref, vt_ref, o_ref, m_ref, alpha_ref, acc_ref, *st_refs):
    blk = ATT_BLOCK
    cw = ATT_QCHUNK
    nblk = q_ref.shape[0] // blk
    chains = [(hd, c) for hd in range(HEADS_PER_STEP) for c in range(blk // cw)]
    assert len(st_refs) == len(chains)

    def head_cols(i):
        hd = chains[i][0]
        return slice(hd * HEAD_PACK, (hd + 1) * HEAD_PACK)

    def state(ref, i):
        hd, c = chains[i]
        return ref.at[hd, :, c * cw:(c + 1) * cw]

    def qk(i, qi, ki, diagonal):
        c = chains[i][1]
        rows = (c + 1) * cw if diagonal else blk
        q = q_ref[pl.ds(pl.multiple_of(qi * blk + c * cw, cw), cw), head_cols(i)]
        k = k_ref[pl.ds(pl.multiple_of(ki * blk, blk), rows), head_cols(i)]
        return lax.dot_general(k, q, _NT_DIMS, preferred_element_type=F32)

    def stage_scores(i, s, diagonal):
        st = st_refs[i]
        if diagonal:
            c = chains[i][1]
            lo, hi = c * cw, (c + 1) * cw
            kv_idx = lax.broadcasted_iota(jnp.int32, (cw, cw), 0)
            q_idx = lax.broadcasted_iota(jnp.int32, (cw, cw), 1)
            tri = jnp.where(kv_idx <= q_idx, s[lo:hi], MASK_VALUE)
            m_new = jnp.max(tri, axis=0, keepdims=True)
            st[lo:hi, :] = tri
            if lo > 0:
                m_new = jnp.maximum(m_new, jnp.max(s[0:lo], axis=0, keepdims=True))
                st[0:lo, :] = s[0:lo]
            if hi < blk:
                st[hi:blk, :] = jnp.full((blk - hi, cw), MASK_VALUE, F32)
            m_old = jnp.full((1, cw), MASK_VALUE, F32)
        else:
            m_old = state(m_ref, i)[...]
            m_new = jnp.maximum(m_old, jnp.max(s, axis=0, keepdims=True))
            st[...] = s
        state(alpha_ref, i)[...] = jnp.exp2(m_old - m_new)
        state(m_ref, i)[...] = m_new

    def accumulate(i, ki, staged_diagonal=False):
        hd, c = chains[i]
        rows = (c + 1) * cw if staged_diagonal else blk
        vt = vt_ref[ki, hd * V_PACK:(hd + 1) * V_PACK, 0:rows]
        p = jnp.exp2(st_refs[i][0:rows, :] - state(m_ref, i)[...]).astype(BF16)
        acc = state(acc_ref, i)
        acc[...] = state(alpha_ref, i)[...] * acc[...] + _dot(vt, p)

    def step(qi, ki, diagonal, staged_ki, staged_diagonal=False):
        s = {j: qk(j, qi, ki, diagonal) for j in range(QK_LOOKAHEAD)}
        for i in range(len(chains)):
            if i + QK_LOOKAHEAD < len(chains):
                s[i + QK_LOOKAHEAD] = qk(i + QK_LOOKAHEAD, qi, ki, diagonal)
            accumulate(i, staged_ki, staged_diagonal)
            stage_scores(i, s.pop(i), diagonal)

    def finalize(qi):
        o_t = jnp.concatenate([acc_ref[hd, 0:V_HEAD, :] / acc_ref[hd, V_HEAD:V_HEAD + 1, :]
                               for hd in range(HEADS_PER_STEP)], axis=0)
        o_ref[pl.ds(pl.multiple_of(qi * blk, blk), blk), :] = o_t.T.astype(o_ref.dtype)

    acc_ref[...] = jnp.zeros(acc_ref.shape, F32)
    for i in range(len(chains)):
        stage_scores(i, qk(i, 0, 0, True), True)

    def q_block(qi, staged_ki):
        step(qi, qi, True, staged_ki)
        finalize(qi - 1)
        step(qi, 0, False, qi, staged_diagonal=True)

        rest = qi - 1
        unroll = ATT_KV_UNROLL

        def kv_group(j, staged):
            for u in range(unroll):
                ki = unroll * j + u + 1
                step(qi, ki, False, staged)
                staged = ki
            return staged

        n_groups = lax.shift_right_logical(rest, unroll.bit_length() - 1)
        staged = lax.fori_loop(0, n_groups, kv_group, 0)

        def kv_single(ki, staged):
            step(qi, ki, False, staged)
            return ki

        return lax.fori_loop(n_groups * unroll + 1, qi, kv_single, staged)

    staged_ki = lax.fori_loop(1, nblk, q_block, 0)
    for i in range(len(chains)):
        accumulate(i, staged_ki)
    finalize(nblk - 1)


def _output_kernel(x1_ref, o_ref, wg_ref, woutb_ref, fnorm_ref, out_ref):
    rc = OUTPUT_ROW_CHUNK
    n_chunks = x1_ref.shape[0] // rc
    rows = lambda c: slice(c * rc, (c + 1) * rc)
    hb, g, y = {}, {}, {}

    def norm_in(c):
        x1 = x1_ref[rows(c), :]
        hb[c] = (x1 * _rms_scale(x1)).astype(BF16)

    def gate_proj(c):
        g[c] = _dot(hb.pop(c), wg_ref[...])

    def gate(c):
        gc = g.pop(c)
        y[c] = (o_ref[rows(c), :].astype(F32) * (gc * jax.nn.sigmoid(gc))).astype(BF16)

    def out_proj_norm(c):
        x2 = x1_ref[rows(c), :] + _dot(y.pop(c), woutb_ref[...])
        out_ref[rows(c), :] = x2 * _rms_scale(x2) * fnorm_ref[...]

    stages = (norm_in, gate_proj, gate, out_proj_norm)
    for t in range(n_chunks + len(stages) - 1):
        for depth, stage in reversed(list(enumerate(stages))):
            if 0 <= t - depth < n_chunks:
                stage(t - depth)


def _const_spec(shape):
    return pl.BlockSpec(shape, lambda *_: (0,) * len(shape))


def _rope_tables(positions):
    inv_freq = ROPE_THETA ** (-jnp.arange(0, QK_ROPE, 2, dtype=F32) / QK_ROPE)
    ang = positions.astype(F32)[:, None, :] * inv_freq[None, :, None]
    cos, sin = jnp.cos(ang), jnp.sin(ang)
    return jnp.concatenate([cos, cos], axis=1), jnp.concatenate([-sin, sin], axis=1)


def _swap_halves(w):
    half = w.shape[-1] // 2
    return jnp.concatenate([w[..., half:], w[..., :half]], axis=-1)


def _prepare_weights(a_norm, a_w_in, kv_norm, w_dkv, ckv_norm, w_ukv, b_norm, b_w_in, q_norm, b_w_uq):
    rows = lambda g: g.astype(F32)[:, None]

    a_w_in = rows(a_norm) * a_w_in
    win = jnp.concatenate(
        [a_w_in[:, part * E_A + j * CONV_CHUNK:part * E_A + (j + 1) * CONV_CHUNK]
         for j in range(N_CONV_CHUNKS) for part in range(4)], axis=-1).astype(BF16)

    w_dkv = rows(kv_norm) * w_dkv
    kr = w_dkv[:, KV_RANK:]
    wdkv = jnp.concatenate([w_dkv[:, :KV_RANK], kr, kr, _swap_halves(kr), _swap_halves(kr)],
                           axis=-1).astype(BF16)

    wk = (rows(ckv_norm) * w_ukv).reshape(KV_RANK, N_HEADS, QK_NOPE + V_HEAD)
    wkn = wk[..., :QK_NOPE].reshape(KV_RANK, N_HEADS * QK_NOPE).astype(BF16)
    pad_v = jnp.zeros((KV_RANK, N_HEADS, V_PACK - V_HEAD), F32)
    wvt = jnp.concatenate([wk[..., QK_NOPE:], pad_v], axis=-1).reshape(KV_RANK, N_HEADS * V_PACK)
    wvt = wvt.T.astype(BF16)

    b_w_in = rows(b_norm) * b_w_in
    wcq = b_w_in[:, :Q_RANK].astype(BF16)
    wg = b_w_in[:, Q_RANK:].astype(BF16)

    wq = (rows(q_norm) * b_w_uq).reshape(Q_RANK, N_HEADS, QK_NOPE + QK_ROPE)
    rope = wq[..., QK_NOPE:]
    wq = jnp.concatenate([wq[..., :QK_NOPE], rope, _swap_halves(rope)], axis=-1)
    wq = wq.reshape(Q_RANK, N_HEADS * HEAD_PACK).astype(BF16)
    return win, wdkv, wkn, wvt, wcq, wg, wq


def kernel(x, positions, a_norm, a_w_in, a_conv, a_w_out, kv_norm, w_dkv, ckv_norm, w_ukv,
           b_norm, b_w_in, b_q_norm, b_w_uq, b_w_out, final_norm):
    bsz, seq, d = x.shape
    assert d == D_MODEL and seq % (TOKEN_SUBTILES * TOKEN_TILE) == 0
    assert seq % ATT_BLOCK == 0 and seq % OUTPUT_TILE == 0
    assert a_norm.shape[0] == 1 and b_norm.shape[0] == 1, "depth-2 trunk: one conv layer, one MLA layer"
    assert a_conv.shape[1:] == (CONV_WIDTH, E_A) and CONV_WIDTH - 1 <= CARRY_ROWS

    cc, ss = _rope_tables(positions)
    win, wdkv, wkn, wvt, wcq, wg, wq = _prepare_weights(
        a_norm[0], a_w_in[0], kv_norm, w_dkv, ckv_norm, w_ukv, b_norm[0], b_w_in[0], b_q_norm[0], b_w_uq[0])
    wouta = a_w_out[0].astype(BF16)
    woutb = b_w_out[0].astype(BF16)

    tm = TOKEN_TILE
    assert tm == ATT_BLOCK, "v'^T is written one attention kv block per token sub-tile"
    n_tiles = seq // tm
    step_rows = TOKEN_SUBTILES * tm
    hp = N_HEADS * HEAD_PACK
    tok_spec = lambda width: pl.BlockSpec((None, step_rows, width), lambda b, s: (b, s, 0))
    rope_spec = pl.BlockSpec((None, QK_ROPE, step_rows), lambda b, s: (b, 0, s))

    x1, q, k, vt = pl.pallas_call(
        _token_kernel,
        grid=(bsz, seq // step_rows),
        in_specs=[
            tok_spec(D_MODEL), rope_spec, rope_spec,
            _const_spec(win.shape), _const_spec(a_conv.shape[1:]), _const_spec(wouta.shape),
            _const_spec(wdkv.shape), _const_spec(wkn.shape), _const_spec(wvt.shape),
            _const_spec(wcq.shape), _const_spec(wq.shape),
        ],
        out_specs=[tok_spec(D_MODEL), tok_spec(hp), tok_spec(hp),
                   pl.BlockSpec((None, TOKEN_SUBTILES, N_HEADS * V_PACK, tm),
                                lambda b, s: (b, s, 0, 0))],
        out_shape=[
            jax.ShapeDtypeStruct((bsz, seq, D_MODEL), F32),
            jax.ShapeDtypeStruct((bsz, seq, hp), BF16),
            jax.ShapeDtypeStruct((bsz, seq, hp), BF16),
            jax.ShapeDtypeStruct((bsz, n_tiles, N_HEADS * V_PACK, tm), BF16),
        ],
        scratch_shapes=[
            pltpu.VMEM((CARRY_ROWS, E_A), F32),
            pltpu.VMEM((step_rows, E_A), BF16),
        ],
        compiler_params=pltpu.CompilerParams(
            dimension_semantics=("arbitrary", "arbitrary"),
            vmem_limit_bytes=VMEM_LIMIT_BYTES),
        name="token_kernel",
    )(x, cc, ss, win, a_conv[0], wouta, wdkv, wkn, wvt, wcq, wq)

    hw = HEADS_PER_STEP * HEAD_PACK
    att_spec = pl.BlockSpec((None, seq, hw), lambda b, h: (b, 0, h))
    o = pl.pallas_call(
        _attention_kernel,
        grid=(bsz, N_HEADS // HEADS_PER_STEP),
        in_specs=[att_spec, att_spec,
                  pl.BlockSpec((None, n_tiles, HEADS_PER_STEP * V_PACK, ATT_BLOCK),
                               lambda b, h: (b, 0, h, 0))],
        out_specs=pl.BlockSpec((None, seq, HEADS_PER_STEP * V_HEAD), lambda b, h: (b, 0, h)),
        out_shape=jax.ShapeDtypeStruct((bsz, seq, E_B), BF16),
        scratch_shapes=[
            pltpu.VMEM((HEADS_PER_STEP, 1, ATT_BLOCK), F32),
            pltpu.VMEM((HEADS_PER_STEP, 1, ATT_BLOCK), F32),
            pltpu.VMEM((HEADS_PER_STEP, V_PACK, ATT_BLOCK), F32),
        ] + [pltpu.VMEM((ATT_BLOCK, ATT_QCHUNK), F32)
             for _ in range(HEADS_PER_STEP * (ATT_BLOCK // ATT_QCHUNK))],
        compiler_params=pltpu.CompilerParams(
            dimension_semantics=("arbitrary", "arbitrary"),
            vmem_limit_bytes=VMEM_LIMIT_BYTES),
        name="attention_kernel",
    )(q, k, vt)

    out_spec = lambda width: pl.BlockSpec((None, OUTPUT_TILE, width), lambda b, s: (b, s, 0))
    out = pl.pallas_call(
        _output_kernel,
        grid=(bsz, seq // OUTPUT_TILE),
        in_specs=[
            out_spec(D_MODEL), out_spec(E_B), _const_spec(wg.shape), _const_spec(woutb.shape),
            _const_spec((1, D_MODEL)),
        ],
        out_specs=out_spec(D_MODEL),
        out_shape=jax.ShapeDtypeStruct((bsz, seq, D_MODEL), F32),
        compiler_params=pltpu.CompilerParams(
            dimension_semantics=("arbitrary", "arbitrary"),
            vmem_limit_bytes=VMEM_LIMIT_BYTES),
        name="output_kernel",
    )(x1, o, wg, woutb, final_norm.reshape(1, D_MODEL).astype(F32))
    return out
```

```python
import math

import jax
import jax.numpy as jnp
from jax import lax
from jax.experimental import pallas as pl
from jax.experimental.pallas import tpu as pltpu

D_MODEL = 1024
E_A = D_MODEL
CONV_WIDTH = 3
N_HEADS = 8
QK_NOPE = 64
QK_ROPE = 32
V_HEAD = 64
KV_RANK = 256
Q_RANK = 384
E_B = N_HEADS * V_HEAD
ROPE_THETA = 10000.0
SOFTMAX_SCALE = 1.0 / math.sqrt(QK_NOPE + QK_ROPE)
EPS = 1e-6

LANES = 128
SUBLANES = 8
BF16_SUBLANES = 2 * SUBLANES
MXU_TILE = 256
V7X_VMEM_BYTES = 64 * 1024 * 1024
HEAD_PACK = LANES
V_PACK = V_HEAD + BF16_SUBLANES
TOKEN_TILE = 512
TOKEN_SUBTILES = 2
OUTPUT_TILE = 2048
OUTPUT_ROW_CHUNK = 256
CONV_CHUNK = 128
N_CONV_CHUNKS = E_A // CONV_CHUNK
CARRY_ROWS = SUBLANES
ATT_BLOCK = 512
ATT_QCHUNK = MXU_TILE
HEADS_PER_STEP = 4
ATT_KV_UNROLL = 2
QK_LOOKAHEAD = 2
MASK_VALUE = float(jnp.finfo(jnp.float32).min)
VMEM_LIMIT_BYTES = V7X_VMEM_BYTES * 7 // 8

BF16 = jnp.bfloat16
F32 = jnp.float32


def _rms_scale(x):
    return lax.rsqrt(jnp.mean(x * x, axis=-1, keepdims=True) + EPS)


def _dot(a, b):
    return jnp.dot(a, b, preferred_element_type=F32)


_NT_DIMS = (((1,), (1,)), ((), ()))


def _token_kernel(x_ref, pos_ref, inv_freq_ref, win_ref, conv_ref, wouta_ref,
                  wdkv_ref, wkn_ref, wvt_ref, wcq_ref, wq_ref,
                  x1_ref, q_ref, k_ref, vt_ref,
                  carry_ref, y_ref):
    @pl.when(pl.program_id(1) == 0)
    def _():
        carry_ref[...] = jnp.zeros((CARRY_ROWS, E_A), F32)

    xn = [_conv_mixer_subtile(sub, x_ref, win_ref, conv_ref, wouta_ref, x1_ref, carry_ref, y_ref)
          for sub in range(TOKEN_SUBTILES)]
    _projections(xn, pos_ref, inv_freq_ref, wdkv_ref, wkn_ref, wvt_ref, wcq_ref, wq_ref, q_ref, k_ref, vt_ref)


def _conv_mixer_subtile(sub, x_ref, win_ref, conv_ref, wouta_ref, x1_ref, carry_ref, y_ref):
    tm = TOKEN_TILE
    cw = CONV_CHUNK
    rows = slice(sub * tm, (sub + 1) * tm)
    x = x_ref[rows, :]
    h = (x * _rms_scale(x)).astype(BF16)

    for j in range(N_CONV_CHUNKS):
        cols = slice(j * cw, (j + 1) * cw)
        pj = _dot(h, win_ref[:, j * 4 * cw:(j + 1) * 4 * cw])
        b = pj[:, 0:cw]
        c = pj[:, cw:2 * cw]
        u = pj[:, 2 * cw:3 * cw]
        g = pj[:, 3 * cw:4 * cw]
        v = c * u
        vv = jnp.concatenate([carry_ref[:, cols], v], axis=0)
        carry_ref[:, cols] = v[tm - CARRY_ROWS:tm]
        conv = None
        for k in range(CONV_WIDTH):
            shift = CONV_WIDTH - 1 - k
            tap = conv_ref[k:k + 1, cols] * (pltpu.roll(vv, shift, axis=0)[CARRY_ROWS:] if shift else v)
            conv = tap if conv is None else conv + tap
        y = (g * jax.nn.sigmoid(g)) * b * conv
        y_ref[rows, cols] = y.astype(BF16)

    x1 = x + _dot(y_ref[rows, :], wouta_ref[...])
    x1_ref[rows, :] = x1
    return (x1 * _rms_scale(x1)).astype(BF16)


def _projections(xn, pos_ref, inv_freq_ref, wdkv_ref, wkn_ref, wvt_ref, wcq_ref, wq_ref, q_ref, k_ref, vt_ref):
    tm = TOKEN_TILE
    subs = range(len(xn))
    rows = [slice(s * tm, (s + 1) * tm) for s in subs]
    lane = lax.broadcasted_iota(jnp.int32, (1, LANES), 1)
    low_half = lane < LANES // 2

    reps = LANES // QK_ROPE
    cc, ss = [], []
    for r in rows:
        ang = inv_freq_ref[...] * pos_ref[:, r].astype(F32)
        cos, sin = jnp.cos(ang), jnp.sin(ang)
        cc.append(jnp.concatenate([cos, cos] * reps, axis=0).T)
        ss.append(jnp.concatenate([-sin, sin] * reps, axis=0).T)

    ckr = [_dot(xn[s], wdkv_ref[...]) for s in subs]
    cq = [_dot(xn[s], wcq_ref[...]) for s in subs]
    rope_terms = [ckr[s][:, KV_RANK:KV_RANK + LANES] * jnp.where(low_half, cc[s], ss[s]) for s in subs]
    rope_k = [t + pltpu.roll(t, LANES // 2, axis=1) for t in rope_terms]
    ckvn = [(ckr[s][:, 0:KV_RANK] * _rms_scale(ckr[s][:, 0:KV_RANK])).astype(BF16) for s in subs]
    cqn = [(c * _rms_scale(c)).astype(BF16) for c in cq]
    kn = [_dot(c, wkn_ref[...]) for c in ckvn]
    q = [_dot(c, wq_ref[...]) for c in cqn]
    head_row = lax.broadcasted_iota(jnp.int32, (N_HEADS * V_PACK, 1), 0) % V_PACK
    ones_row = jnp.where(head_row == V_HEAD, 1.0, 0.0)
    for s in subs:
        vt = lax.dot_general(wvt_ref[...], ckvn[s], _NT_DIMS, preferred_element_type=F32)
        vt_ref[s] = (vt + ones_row).astype(BF16)

    qscale = SOFTMAX_SCALE * math.log2(math.e)
    for s in subs:
        tq = jnp.where(lane < QK_NOPE, 1.0, jnp.where(lane < QK_NOPE + QK_ROPE, cc[s], ss[s])) * qscale
        for hd in range(N_HEADS):
            hc = slice(hd * HEAD_PACK, (hd + 1) * HEAD_PACK)
            kn_pair = kn[s][:, (hd // 2) * LANES:(hd // 2 + 1) * LANES]
            if hd % 2 == 1:
                kn_pair = pltpu.roll(kn_pair, LANES // 2, axis=1)
            k_ref[rows[s], hc] = jnp.where(low_half, kn_pair, rope_k[s]).astype(BF16)
            q_ref[rows[s], hc] = (q[s][:, hc] * tq).astype(BF16)


def _attention_kernel(q_ref, k_ref, vt_ref, o_ref, m_ref, alpha_ref, acc_ref, *st_refs):
    blk = ATT_BLOCK
    cw = ATT_QCHUNK
    nblk = q_ref.shape[0] // blk
    chains = [(hd, c) for hd in range(HEADS_PER_STEP) for c in range(blk // cw)]
    assert len(st_refs) == len(chains)

    def head_cols(i):
        hd = chains[i][0]
        return slice(hd * HEAD_PACK, (hd + 1) * HEAD_PACK)

    def state(ref, i):
        hd, c = chains[i]
        return ref.at[hd, :, c * cw:(c + 1) * cw]

    def qk(i, qi, ki, diagonal):
        c = chains[i][1]
        rows = (c + 1) * cw if diagonal else blk
        q = q_ref[pl.ds(pl.multiple_of(qi * blk + c * cw, cw), cw), head_cols(i)]
        k = k_ref[pl.ds(pl.multiple_of(ki * blk, blk), rows), head_cols(i)]
        return lax.dot_general(k, q, _NT_DIMS, preferred_element_type=F32)

    def stage_scores(i, s, diagonal):
        st = st_refs[i]
        if diagonal:
            c = chains[i][1]
            lo, hi = c * cw, (c + 1) * cw
            kv_idx = lax.broadcasted_iota(jnp.int32, (cw, cw), 0)
            q_idx = lax.broadcasted_iota(jnp.int32, (cw, cw), 1)
            tri = jnp.where(kv_idx <= q_idx, s[lo:hi], MASK_VALUE)
            m_new = jnp.max(tri, axis=0, keepdims=True)
            st[lo:hi, :] = tri
            if lo > 0:
                m_new = jnp.maximum(m_new, jnp.max(s[0:lo], axis=0, keepdims=True))
                st[0:lo, :] = s[0:lo]
            if hi < blk:
                st[hi:blk, :] = jnp.full((blk - hi, cw), MASK_VALUE, F32)
            m_old = jnp.full((1, cw), MASK_VALUE, F32)
        else:
            m_old = state(m_ref, i)[...]
            m_new = jnp.maximum(m_old, jnp.max(s, axis=0, keepdims=True))
            st[...] = s
        state(alpha_ref, i)[...] = jnp.exp2(m_old - m_new)
        state(m_ref, i)[...] = m_new

    def accumulate(i, ki, staged_diagonal=False):
        hd, c = chains[i]
        rows = (c + 1) * cw if staged_diagonal else blk
        vt = vt_ref[ki, hd * V_PACK:(hd + 1) * V_PACK, 0:rows]
        p = jnp.exp2(st_refs[i][0:rows, :] - state(m_ref, i)[...]).astype(BF16)
        acc = state(acc_ref, i)
        acc[...] = state(alpha_ref, i)[...] * acc[...] + _dot(vt, p)

    def step(qi, ki, diagonal, staged_ki, staged_diagonal=False):
        s = {j: qk(j, qi, ki, diagonal) for j in range(QK_LOOKAHEAD)}
        for i in range(len(chains)):
            if i + QK_LOOKAHEAD < len(chains):
                s[i + QK_LOOKAHEAD] = qk(i + QK_LOOKAHEAD, qi, ki, diagonal)
            accumulate(i, staged_ki, staged_diagonal)
            stage_scores(i, s.pop(i), diagonal)

    def finalize(qi):
        o_t = jnp.concatenate([acc_ref[hd, 0:V_HEAD, :] / acc_ref[hd, V_HEAD:V_HEAD + 1, :]
                               for hd in range(HEADS_PER_STEP)], axis=0)
        o_ref[pl.ds(pl.multiple_of(qi * blk, blk), blk), :] = o_t.T.astype(o_ref.dtype)

    acc_ref[...] = jnp.zeros(acc_ref.shape, F32)
    for i in range(len(chains)):
        stage_scores(i, qk(i, 0, 0, True), True)

    def q_block(qi, staged_ki):
        step(qi, qi, True, staged_ki)
        finalize(qi - 1)
        step(qi, 0, False, qi, staged_diagonal=True)

        rest = qi - 1
        unroll = ATT_KV_UNROLL

        def kv_group(j, staged):
            for u in range(unroll):
                ki = unroll * j + u + 1
                step(qi, ki, False, staged)
                staged = ki
            return staged

        n_groups = lax.shift_right_logical(rest, unroll.bit_length() - 1)
        staged = lax.fori_loop(0, n_groups, kv_group, 0)

        def kv_single(ki, staged):
            step(qi, ki, False, staged)
            return ki

        return lax.fori_loop(n_groups * unroll + 1, qi, kv_single, staged)

    staged_ki = lax.fori_loop(1, nblk, q_block, 0)
    for i in range(len(chains)):
        accumulate(i, staged_ki)
    finalize(nblk - 1)


def _output_kernel(x1_ref, o_ref, wg_ref, woutb_ref, fnorm_ref, out_ref):
    rc = OUTPUT_ROW_CHUNK
    n_chunks = x1_ref.shape[0] // rc
    rows = lambda c: slice(c * rc, (c + 1) * rc)
    hb, g, y = {}, {}, {}

    def norm_in(c):
        x1 = x1_ref[rows(c), :]
        hb[c] = (x1 * _rms_scale(x1)).astype(BF16)

    def gate_proj(c):
        g[c] = _dot(hb.pop(c), wg_ref[...])

    def gate(c):
        gc = g.pop(c)
        y[c] = (o_ref[rows(c), :].astype(F32) * (gc * jax.nn.sigmoid(gc))).astype(BF16)

    def out_proj_norm(c):
        x2 = x1_ref[rows(c), :] + _dot(y.pop(c), woutb_ref[...])
        out_ref[rows(c), :] = x2 * _rms_scale(x2) * fnorm_ref[...]

    stages = (norm_in, gate_proj, gate, out_proj_norm)
    for t in range(n_chunks + len(stages) - 1):
        for depth, stage in reversed(list(enumerate(stages))):
            if 0 <= t - depth < n_chunks:
                stage(t - depth)


def _const_spec(shape):
    return pl.BlockSpec(shape, lambda *_: (0,) * len(shape))


def _swap_halves(w):
    half = w.shape[-1] // 2
    return jnp.concatenate([w[..., half:], w[..., :half]], axis=-1)


def _prepare_weights(a_norm, a_w_in, kv_norm, w_dkv, ckv_norm, w_ukv, b_norm, b_w_in, q_norm, b_w_uq):
    rows = lambda g: g.astype(F32)[:, None]

    a_w_in = rows(a_norm) * a_w_in
    win = jnp.concatenate(
        [a_w_in[:, part * E_A + j * CONV_CHUNK:part * E_A + (j + 1) * CONV_CHUNK]
         for j in range(N_CONV_CHUNKS) for part in range(4)], axis=-1).astype(BF16)

    w_dkv = rows(kv_norm) * w_dkv
    kr = w_dkv[:, KV_RANK:]
    wdkv = jnp.concatenate([w_dkv[:, :KV_RANK], kr, kr, _swap_halves(kr), _swap_halves(kr)],
                           axis=-1).astype(BF16)

    wk = (rows(ckv_norm) * w_ukv).reshape(KV_RANK, N_HEADS, QK_NOPE + V_HEAD)
    wkn = wk[..., :QK_NOPE].reshape(KV_RANK, N_HEADS * QK_NOPE).astype(BF16)
    pad_v = jnp.zeros((KV_RANK, N_HEADS, V_PACK - V_HEAD), F32)
    wvt = jnp.concatenate([wk[..., QK_NOPE:], pad_v], axis=-1).reshape(KV_RANK, N_HEADS * V_PACK)
    wvt = wvt.T.astype(BF16)

    b_w_in = rows(b_norm) * b_w_in
    wcq = b_w_in[:, :Q_RANK].astype(BF16)
    wg = b_w_in[:, Q_RANK:].astype(BF16)

    wq = (rows(q_norm) * b_w_uq).reshape(Q_RANK, N_HEADS, QK_NOPE + QK_ROPE)
    rope = wq[..., QK_NOPE:]
    wq = jnp.concatenate([wq[..., :QK_NOPE], rope, _swap_halves(rope)], axis=-1)
    wq = wq.reshape(Q_RANK, N_HEADS * HEAD_PACK).astype(BF16)
    return win, wdkv, wkn, wvt, wcq, wg, wq


def kernel(x, positions, a_norm, a_w_in, a_conv, a_w_out, kv_norm, w_dkv, ckv_norm, w_ukv,
           b_norm, b_w_in, b_q_norm, b_w_uq, b_w_out, final_norm):
    bsz, seq, d = x.shape
    assert d == D_MODEL and seq % (TOKEN_SUBTILES * TOKEN_TILE) == 0
    assert seq % ATT_BLOCK == 0 and seq % OUTPUT_TILE == 0
    assert a_norm.shape[0] == 1 and b_norm.shape[0] == 1, "depth-2 trunk: one conv layer, one MLA layer"
    assert a_conv.shape[1:] == (CONV_WIDTH, E_A) and CONV_WIDTH - 1 <= CARRY_ROWS

    inv_freq = ROPE_THETA ** (-jnp.arange(0, QK_ROPE, 2, dtype=F32) / QK_ROPE)
    inv_freq = inv_freq.reshape(QK_ROPE // 2, 1)
    pos = positions.reshape(bsz, 1, seq)
    win, wdkv, wkn, wvt, wcq, wg, wq = _prepare_weights(
        a_norm[0], a_w_in[0], kv_norm, w_dkv, ckv_norm, w_ukv, b_norm[0], b_w_in[0], b_q_norm[0], b_w_uq[0])
    wouta = a_w_out[0].astype(BF16)
    woutb = b_w_out[0].astype(BF16)

    tm = TOKEN_TILE
    assert tm == ATT_BLOCK, "v'^T is written one attention kv block per token sub-tile"
    n_tiles = seq // tm
    step_rows = TOKEN_SUBTILES * tm
    hp = N_HEADS * HEAD_PACK
    tok_spec = lambda width: pl.BlockSpec((None, step_rows, width), lambda b, s: (b, s, 0))
    pos_spec = pl.BlockSpec((None, 1, step_rows), lambda b, s: (b, 0, s))

    x1, q, k, vt = pl.pallas_call(
        _token_kernel,
        grid=(bsz, seq // step_rows),
        in_specs=[
            tok_spec(D_MODEL), pos_spec, _const_spec(inv_freq.shape),
            _const_spec(win.shape), _const_spec(a_conv.shape[1:]), _const_spec(wouta.shape),
            _const_spec(wdkv.shape), _const_spec(wkn.shape), _const_spec(wvt.shape),
            _const_spec(wcq.shape), _const_spec(wq.shape),
        ],
        out_specs=[tok_spec(D_MODEL), tok_spec(hp), tok_spec(hp),
                   pl.BlockSpec((None, TOKEN_SUBTILES, N_HEADS * V_PACK, tm),
                                lambda b, s: (b, s, 0, 0))],
        out_shape=[
            jax.ShapeDtypeStruct((bsz, seq, D_MODEL), F32),
            jax.ShapeDtypeStruct((bsz, seq, hp), BF16),
            jax.ShapeDtypeStruct((bsz, seq, hp), BF16),
            jax.ShapeDtypeStruct((bsz, n_tiles, N_HEADS * V_PACK, tm), BF16),
        ],
        scratch_shapes=[
            pltpu.VMEM((CARRY_ROWS, E_A), F32),
            pltpu.VMEM((step_rows, E_A), BF16),
        ],
        compiler_params=pltpu.CompilerParams(
            dimension_semantics=("arbitrary", "arbitrary"),
            vmem_limit_bytes=VMEM_LIMIT_BYTES),
        name="token_kernel",
    )(x, pos, inv_freq, win, a_conv[0], wouta, wdkv, wkn, wvt, wcq, wq)

    hw = HEADS_PER_STEP * HEAD_PACK
    att_spec = pl.BlockSpec((None, seq, hw), lambda b, h: (b, 0, h))
    o = pl.pallas_call(
        _attention_kernel,
        grid=(bsz, N_HEADS // HEADS_PER_STEP),
        in_specs=[att_spec, att_spec,
                  pl.BlockSpec((None, n_tiles, HEADS_PER_STEP * V_PACK, ATT_BLOCK),
                               lambda b, h: (b, 0, h, 0))],
        out_specs=pl.BlockSpec((None, seq, HEADS_PER_STEP * V_HEAD), lambda b, h: (b, 0, h)),
        out_shape=jax.ShapeDtypeStruct((bsz, seq, E_B), BF16),
        scratch_shapes=[
            pltpu.VMEM((HEADS_PER_STEP, 1, ATT_BLOCK), F32),
            pltpu.VMEM((HEADS_PER_STEP, 1, ATT_BLOCK), F32),
            pltpu.VMEM((HEADS_PER_STEP, V_PACK, ATT_BLOCK), F32),
        ] + [pltpu.VMEM((ATT_BLOCK, ATT_QCHUNK), F32)
             for _ in range(HEADS_PER_STEP * (ATT_BLOCK // ATT_QCHUNK))],
        compiler_params=pltpu.CompilerParams(
            dimension_semantics=("arbitrary", "arbitrary"),
            vmem_limit_bytes=VMEM_LIMIT_BYTES),
        name="attention_kernel",
    )(q, k, vt)

    out_spec = lambda width: pl.BlockSpec((None, OUTPUT_TILE, width), lambda b, s: (b, s, 0))
    out = pl.pallas_call(
        _output_kernel,
        grid=(bsz, seq // OUTPUT_TILE),
        in_specs=[
            out_spec(D_MODEL), out_spec(E_B), _const_spec(wg.shape), _const_spec(woutb.shape),
            _const_spec((1, D_MODEL)),
        ],
        out_specs=out_spec(D_MODEL),
        out_shape=jax.ShapeDtypeStruct((bsz, seq, D_MODEL), F32),
        compiler_params=pltpu.CompilerParams(
            dimension_semantics=("arbitrary", "arbitrary"),
            vmem_limit_bytes=VMEM_LIMIT_BYTES),
        name="output_kernel",
    )(x1, o, wg, woutb, final_norm.reshape(1, D_MODEL).astype(F32))
    return out
```

```python
import math

import jax
import jax.numpy as jnp
from jax import lax
from jax.experimental import pallas as pl
from jax.experimental.pallas import tpu as pltpu

D_MODEL = 1024
E_A = D_MODEL
CONV_WIDTH = 3
N_HEADS = 8
QK_NOPE = 64
QK_ROPE = 32
V_HEAD = 64
KV_RANK = 256
Q_RANK = 384
E_B = N_HEADS * V_HEAD
ROPE_THETA = 10000.0
SOFTMAX_SCALE = 1.0 / math.sqrt(QK_NOPE + QK_ROPE)
EPS = 1e-6

LANES = 128
SUBLANES = 8
BF16_SUBLANES = 2 * SUBLANES
MXU_TILE = 256
V7X_VMEM_BYTES = 64 * 1024 * 1024
HEAD_PACK = LANES
V_PACK = V_HEAD + BF16_SUBLANES
TOKEN_TILE = 512
TOKEN_SUBTILES = 2
MIXER_TILE = 1024
OUTPUT_TILE = 2048
OUTPUT_ROW_CHUNK = 256
CONV_CHUNK = 128
N_CONV_CHUNKS = E_A // CONV_CHUNK
CARRY_ROWS = SUBLANES
ATT_BLOCK = 512
ATT_QCHUNK = MXU_TILE
HEADS_PER_STEP = 4
ATT_KV_UNROLL = 2
QK_LOOKAHEAD = 2
MASK_VALUE = float(jnp.finfo(jnp.float32).min)
VMEM_LIMIT_BYTES = V7X_VMEM_BYTES * 7 // 8

BF16 = jnp.bfloat16
F32 = jnp.float32


def _rms_scale(x):
    return lax.rsqrt(jnp.mean(x * x, axis=-1, keepdims=True) + EPS)


def _dot(a, b):
    return jnp.dot(a, b, preferred_element_type=F32)


_NT_DIMS = (((1,), (1,)), ((), ()))


def _token_kernel(x_ref, pos_ref, inv_freq_ref, win_ref, conv_ref, wouta_ref,
                  wdkv_ref, wkn_ref, wvt_ref, wcq_ref, wq_ref,
                  x1_ref, q_ref, k_ref, vt_ref,
                  carry_ref, y_ref):
    @pl.when(pl.program_id(1) == 0)
    def _():
        carry_ref[...] = jnp.zeros((CARRY_ROWS, E_A), F32)

    step_rows = TOKEN_SUBTILES * TOKEN_TILE
    xn_all = jnp.concatenate(
        [_conv_mixer_tile(t, x_ref, win_ref, conv_ref, wouta_ref, x1_ref, carry_ref, y_ref)
         for t in range(step_rows // MIXER_TILE)], axis=0)
    xn = [xn_all[s * TOKEN_TILE:(s + 1) * TOKEN_TILE] for s in range(TOKEN_SUBTILES)]
    _projections(xn, pos_ref, inv_freq_ref, wdkv_ref, wkn_ref, wvt_ref, wcq_ref, wq_ref, q_ref, k_ref, vt_ref)


def _conv_mixer_tile(t, x_ref, win_ref, conv_ref, wouta_ref, x1_ref, carry_ref, y_ref):
    tm = MIXER_TILE
    cw = CONV_CHUNK
    rows = slice(t * tm, (t + 1) * tm)
    x = x_ref[rows, :]
    h = (x * _rms_scale(x)).astype(BF16)

    for j in range(N_CONV_CHUNKS):
        cols = slice(j * cw, (j + 1) * cw)
        pj = _dot(h, win_ref[:, j * 4 * cw:(j + 1) * 4 * cw])
        b = pj[:, 0:cw]
        c = pj[:, cw:2 * cw]
        u = pj[:, 2 * cw:3 * cw]
        g = pj[:, 3 * cw:4 * cw]
        v = c * u
        vv = jnp.concatenate([carry_ref[:, cols], v], axis=0)
        carry_ref[:, cols] = v[tm - CARRY_ROWS:tm]
        conv = None
        for k in range(CONV_WIDTH):
            shift = CONV_WIDTH - 1 - k
            tap = conv_ref[k:k + 1, cols] * (pltpu.roll(vv, shift, axis=0)[CARRY_ROWS:] if shift else v)
            conv = tap if conv is None else conv + tap
        y = (g * jax.nn.sigmoid(g)) * b * conv
        y_ref[rows, cols] = y.astype(BF16)

    x1 = x + _dot(y_ref[rows, :], wouta_ref[...])
    x1_ref[rows, :] = x1
    return (x1 * _rms_scale(x1)).astype(BF16)


def _projections(xn, pos_ref, inv_freq_ref, wdkv_ref, wkn_ref, wvt_ref, wcq_ref, wq_ref, q_ref, k_ref, vt_ref):
    tm = TOKEN_TILE
    subs = range(len(xn))
    rows = [slice(s * tm, (s + 1) * tm) for s in subs]
    lane = lax.broadcasted_iota(jnp.int32, (1, LANES), 1)
    low_half = lane < LANES // 2

    reps = LANES // QK_ROPE
    cc, ss = [], []
    for r in rows:
        ang = inv_freq_ref[...] * pos_ref[:, r].astype(F32)
        cos, sin = jnp.cos(ang), jnp.sin(ang)
        cc.append(jnp.concatenate([cos, cos] * reps, axis=0).T)
        ss.append(jnp.concatenate([-sin, sin] * reps, axis=0).T)

    ckr = [_dot(xn[s], wdkv_ref[...]) for s in subs]
    cq = [_dot(xn[s], wcq_ref[...]) for s in subs]
    rope_terms = [ckr[s][:, KV_RANK:KV_RANK + LANES] * jnp.where(low_half, cc[s], ss[s]) for s in subs]
    rope_k = [t + pltpu.roll(t, LANES // 2, axis=1) for t in rope_terms]
    ckvn = [(ckr[s][:, 0:KV_RANK] * _rms_scale(ckr[s][:, 0:KV_RANK])).astype(BF16) for s in subs]
    cqn = [(c * _rms_scale(c)).astype(BF16) for c in cq]
    kn = [_dot(c, wkn_ref[...]) for c in ckvn]
    q = [_dot(c, wq_ref[...]) for c in cqn]
    head_row = lax.broadcasted_iota(jnp.int32, (N_HEADS * V_PACK, 1), 0) % V_PACK
    ones_row = jnp.where(head_row == V_HEAD, 1.0, 0.0)
    for s in subs:
        vt = lax.dot_general(wvt_ref[...], ckvn[s], _NT_DIMS, preferred_element_type=F32)
        vt_ref[s] = (vt + ones_row).astype(BF16)

    qscale = SOFTMAX_SCALE * math.log2(math.e)
    for s in subs:
        tq = jnp.where(lane < QK_NOPE, 1.0, jnp.where(lane < QK_NOPE + QK_ROPE, cc[s], ss[s])) * qscale
        for hd in range(N_HEADS):
            hc = slice(hd * HEAD_PACK, (hd + 1) * HEAD_PACK)
            kn_pair = kn[s][:, (hd // 2) * LANES:(hd // 2 + 1) * LANES]
            if hd % 2 == 1:
                kn_pair = pltpu.roll(kn_pair, LANES // 2, axis=1)
            k_ref[rows[s], hc] = jnp.where(low_half, kn_pair, rope_k[s]).astype(BF16)
            q_ref[rows[s], hc] = (q[s][:, hc] * tq).astype(BF16)


def _attention_kernel(q_ref, k_ref, vt_ref, o_ref, m_ref, alpha_ref, acc_ref, *st_refs):
    blk = ATT_BLOCK
    cw = ATT_QCHUNK
    nblk = q_ref.shape[0] // blk
    chains = [(hd, c) for hd in range(HEADS_PER_STEP) for c in range(blk // cw)]
    assert len(st_refs) == len(chains)

    def head_cols(i):
        hd = chains[i][0]
        return slice(hd * HEAD_PACK, (hd + 1) * HEAD_PACK)

    def state(ref, i):
        hd, c = chains[i]
        return ref.at[hd, :, c * cw:(c + 1) * cw]

    def qk(i, qi, ki, diagonal):
        c = chains[i][1]
        rows = (c + 1) * cw if diagonal else blk
        q = q_ref[pl.ds(pl.multiple_of(qi * blk + c * cw, cw), cw), head_cols(i)]
        k = k_ref[pl.ds(pl.multiple_of(ki * blk, blk), rows), head_cols(i)]
        return lax.dot_general(k, q, _NT_DIMS, preferred_element_type=F32)

    def stage_scores(i, s, diagonal):
        st = st_refs[i]
        if diagonal:
            c = chains[i][1]
            lo, hi = c * cw, (c + 1) * cw
            kv_idx = lax.broadcasted_iota(jnp.int32, (cw, cw), 0)
            q_idx = lax.broadcasted_iota(jnp.int32, (cw, cw), 1)
            tri = jnp.where(kv_idx <= q_idx, s[lo:hi], MASK_VALUE)
            m_new = jnp.max(tri, axis=0, keepdims=True)
            st[lo:hi, :] = tri
            if lo > 0:
                m_new = jnp.maximum(m_new, jnp.max(s[0:lo], axis=0, keepdims=True))
                st[0:lo, :] = s[0:lo]
            if hi < blk:
                st[hi:blk, :] = jnp.full((blk - hi, cw), MASK_VALUE, F32)
            m_old = jnp.full((1, cw), MASK_VALUE, F32)
        else:
            m_old = state(m_ref, i)[...]
            m_new = jnp.maximum(m_old, jnp.max(s, axis=0, keepdims=True))
            st[...] = s
        state(alpha_ref, i)[...] = jnp.exp2(m_old - m_new)
        state(m_ref, i)[...] = m_new

    def accumulate(i, ki, staged_diagonal=False):
        hd, c = chains[i]
        rows = (c + 1) * cw if staged_diagonal else blk
        vt = vt_ref[ki, hd * V_PACK:(hd + 1) * V_PACK, 0:rows]
        p = jnp.exp2(st_refs[i][0:rows, :] - state(m_ref, i)[...]).astype(BF16)
        acc = state(acc_ref, i)
        acc[...] = state(alpha_ref, i)[...] * acc[...] + _dot(vt, p)

    def step(qi, ki, diagonal, staged_ki, staged_diagonal=False):
        s = {j: qk(j, qi, ki, diagonal) for j in range(QK_LOOKAHEAD)}
        for i in range(len(chains)):
            if i + QK_LOOKAHEAD < len(chains):
                s[i + QK_LOOKAHEAD] = qk(i + QK_LOOKAHEAD, qi, ki, diagonal)
            accumulate(i, staged_ki, staged_diagonal)
            stage_scores(i, s.pop(i), diagonal)

    def finalize(qi):
        o_t = jnp.concatenate([acc_ref[hd, 0:V_HEAD, :] / acc_ref[hd, V_HEAD:V_HEAD + 1, :]
                               for hd in range(HEADS_PER_STEP)], axis=0)
        o_ref[pl.ds(pl.multiple_of(qi * blk, blk), blk), :] = o_t.T.astype(o_ref.dtype)

    acc_ref[...] = jnp.zeros(acc_ref.shape, F32)
    for i in range(len(chains)):
        stage_scores(i, qk(i, 0, 0, True), True)

    def q_block(qi, staged_ki):
        step(qi, qi, True, staged_ki)
        finalize(qi - 1)
        step(qi, 0, False, qi, staged_diagonal=True)

        rest = qi - 1
        unroll = ATT_KV_UNROLL

        def kv_group(j, staged):
            for u in range(unroll):
                ki = unroll * j + u + 1
                step(qi, ki, False, staged)
                staged = ki
            return staged

        n_groups = lax.shift_right_logical(rest, unroll.bit_length() - 1)
        staged = lax.fori_loop(0, n_groups, kv_group, 0)

        def kv_single(ki, staged):
            step(qi, ki, False, staged)
            return ki

        return lax.fori_loop(n_groups * unroll + 1, qi, kv_single, staged)

    staged_ki = lax.fori_loop(1, nblk, q_block, 0)
    for i in range(len(chains)):
        accumulate(i, staged_ki)
    finalize(nblk - 1)


def _output_kernel(x1_ref, o_ref, wg_ref, woutb_ref, fnorm_ref, out_ref):
    rc = OUTPUT_ROW_CHUNK
    n_chunks = x1_ref.shape[0] // rc
    rows = lambda c: slice(c * rc, (c + 1) * rc)
    hb, g, y = {}, {}, {}

    def norm_in(c):
        x1 = x1_ref[rows(c), :]
        hb[c] = (x1 * _rms_scale(x1)).astype(BF16)

    def gate_proj(c):
        g[c] = _dot(hb.pop(c), wg_ref[...])

    def gate(c):
        gc = g.pop(c)
        y[c] = (o_ref[rows(c), :].astype(F32) * (gc * jax.nn.sigmoid(gc))).astype(BF16)

    def out_proj_norm(c):
        x2 = x1_ref[rows(c), :] + _dot(y.pop(c), woutb_ref[...])
        out_ref[rows(c), :] = x2 * _rms_scale(x2) * fnorm_ref[...]

    stages = (norm_in, gate_proj, gate, out_proj_norm)
    for t in range(n_chunks + len(stages) - 1):
        for depth, stage in reversed(list(enumerate(stages))):
            if 0 <= t - depth < n_chunks:
                stage(t - depth)


def _const_spec(shape):
    return pl.BlockSpec(shape, lambda *_: (0,) * len(shape))


def _swap_halves(w):
    half = w.shape[-1] // 2
    return jnp.concatenate([w[..., half:], w[..., :half]], axis=-1)


def _prepare_weights(a_norm, a_w_in, kv_norm, w_dkv, ckv_norm, w_ukv, b_norm, b_w_in, q_norm, b_w_uq):
    rows = lambda g: g.astype(F32)[:, None]

    a_w_in = rows(a_norm) * a_w_in
    win = jnp.concatenate(
        [a_w_in[:, part * E_A + j * CONV_CHUNK:part * E_A + (j + 1) * CONV_CHUNK]
         for j in range(N_CONV_CHUNKS) for part in range(4)], axis=-1).astype(BF16)

    w_dkv = rows(kv_norm) * w_dkv
    kr = w_dkv[:, KV_RANK:]
    wdkv = jnp.concatenate([w_dkv[:, :KV_RANK], kr, kr, _swap_halves(kr), _swap_halves(kr)],
                           axis=-1).astype(BF16)

    wk = (rows(ckv_norm) * w_ukv).reshape(KV_RANK, N_HEADS, QK_NOPE + V_HEAD)
    wkn = wk[..., :QK_NOPE].reshape(KV_RANK, N_HEADS * QK_NOPE).astype(BF16)
    pad_v = jnp.zeros((KV_RANK, N_HEADS, V_PACK - V_HEAD), F32)
    wvt = jnp.concatenate([wk[..., QK_NOPE:], pad_v], axis=-1).reshape(KV_RANK, N_HEADS * V_PACK)
    wvt = wvt.T.astype(BF16)

    b_w_in = rows(b_norm) * b_w_in
    wcq = b_w_in[:, :Q_RANK].astype(BF16)
    wg = b_w_in[:, Q_RANK:].astype(BF16)

    wq = (rows(q_norm) * b_w_uq).reshape(Q_RANK, N_HEADS, QK_NOPE + QK_ROPE)
    rope = wq[..., QK_NOPE:]
    wq = jnp.concatenate([wq[..., :QK_NOPE], rope, _swap_halves(rope)], axis=-1)
    wq = wq.reshape(Q_RANK, N_HEADS * HEAD_PACK).astype(BF16)
    return win, wdkv, wkn, wvt, wcq, wg, wq


def kernel(x, positions, a_norm, a_w_in, a_conv, a_w_out, kv_norm, w_dkv, ckv_norm, w_ukv,
           b_norm, b_w_in, b_q_norm, b_w_uq, b_w_out, final_norm):
    bsz, seq, d = x.shape
    assert d == D_MODEL and seq % (TOKEN_SUBTILES * TOKEN_TILE) == 0
    assert seq % ATT_BLOCK == 0 and seq % OUTPUT_TILE == 0
    assert a_norm.shape[0] == 1 and b_norm.shape[0] == 1, "depth-2 trunk: one conv layer, one MLA layer"
    assert a_conv.shape[1:] == (CONV_WIDTH, E_A) and CONV_WIDTH - 1 <= CARRY_ROWS

    inv_freq = ROPE_THETA ** (-jnp.arange(0, QK_ROPE, 2, dtype=F32) / QK_ROPE)
    inv_freq = inv_freq.reshape(QK_ROPE // 2, 1)
    pos = positions.reshape(bsz, 1, seq)
    win, wdkv, wkn, wvt, wcq, wg, wq = _prepare_weights(
        a_norm[0], a_w_in[0], kv_norm, w_dkv, ckv_norm, w_ukv, b_norm[0], b_w_in[0], b_q_norm[0], b_w_uq[0])
    wouta = a_w_out[0].astype(BF16)
    woutb = b_w_out[0].astype(BF16)

    tm = TOKEN_TILE
    assert tm == ATT_BLOCK, "v'^T is written one attention kv block per token sub-tile"
    n_tiles = seq // tm
    step_rows = TOKEN_SUBTILES * tm
    hp = N_HEADS * HEAD_PACK
    tok_spec = lambda width: pl.BlockSpec((None, step_rows, width), lambda b, s: (b, s, 0))
    pos_spec = pl.BlockSpec((None, 1, step_rows), lambda b, s: (b, 0, s))

    x1, q, k, vt = pl.pallas_call(
        _token_kernel,
        grid=(bsz, seq // step_rows),
        in_specs=[
            tok_spec(D_MODEL), pos_spec, _const_spec(inv_freq.shape),
            _const_spec(win.shape), _const_spec(a_conv.shape[1:]), _const_spec(wouta.shape),
            _const_spec(wdkv.shape), _const_spec(wkn.shape), _const_spec(wvt.shape),
            _const_spec(wcq.shape), _const_spec(wq.shape),
        ],
        out_specs=[tok_spec(D_MODEL), tok_spec(hp), tok_spec(hp),
                   pl.BlockSpec((None, TOKEN_SUBTILES, N_HEADS * V_PACK, tm),
                                lambda b, s: (b, s, 0, 0))],
        out_shape=[
            jax.ShapeDtypeStruct((bsz, seq, D_MODEL), F32),
            jax.ShapeDtypeStruct((bsz, seq, hp), BF16),
            jax.ShapeDtypeStruct((bsz, seq, hp), BF16),
            jax.ShapeDtypeStruct((bsz, n_tiles, N_HEADS * V_PACK, tm), BF16),
        ],
        scratch_shapes=[
            pltpu.VMEM((CARRY_ROWS, E_A), F32),
            pltpu.VMEM((step_rows, E_A), BF16),
        ],
        compiler_params=pltpu.CompilerParams(
            dimension_semantics=("arbitrary", "arbitrary"),
            vmem_limit_bytes=VMEM_LIMIT_BYTES),
        name="token_kernel",
    )(x, pos, inv_freq, win, a_conv[0], wouta, wdkv, wkn, wvt, wcq, wq)

    hw = HEADS_PER_STEP * HEAD_PACK
    att_spec = pl.BlockSpec((None, seq, hw), lambda b, h: (b, 0, h))
    o = pl.pallas_call(
        _attention_kernel,
        grid=(bsz, N_HEADS // HEADS_PER_STEP),
        in_specs=[att_spec, att_spec,
                  pl.BlockSpec((None, n_tiles, HEADS_PER_STEP * V_PACK, ATT_BLOCK),
                               lambda b, h: (b, 0, h, 0))],
        out_specs=pl.BlockSpec((None, seq, HEADS_PER_STEP * V_HEAD), lambda b, h: (b, 0, h)),
        out_shape=jax.ShapeDtypeStruct((bsz, seq, E_B), BF16),
        scratch_shapes=[
            pltpu.VMEM((HEADS_PER_STEP, 1, ATT_BLOCK), F32),
            pltpu.VMEM((HEADS_PER_STEP, 1, ATT_BLOCK), F32),
            pltpu.VMEM((HEADS_PER_STEP, V_PACK, ATT_BLOCK), F32),
        ] + [pltpu.VMEM((ATT_BLOCK, ATT_QCHUNK), F32)
             for _ in range(HEADS_PER_STEP * (ATT_BLOCK // ATT_QCHUNK))],
        compiler_params=pltpu.CompilerParams(
            dimension_semantics=("arbitrary", "arbitrary"),
            vmem_limit_bytes=VMEM_LIMIT_BYTES),
        name="attention_kernel",
    )(q, k, vt)

    out_spec = lambda width: pl.BlockSpec((None, OUTPUT_TILE, width), lambda b, s: (b, s, 0))
    out = pl.pallas_call(
        _output_kernel,
        grid=(bsz, seq // OUTPUT_TILE),
        in_specs=[
            out_spec(D_MODEL), out_spec(E_B), _const_spec(wg.shape), _const_spec(woutb.shape),
            _const_spec((1, D_MODEL)),
        ],
        out_specs=out_spec(D_MODEL),
        out_shape=jax.ShapeDtypeStruct((bsz, seq, D_MODEL), F32),
        compiler_params=pltpu.CompilerParams(
            dimension_semantics=("arbitrary", "arbitrary"),
            vmem_limit_bytes=VMEM_LIMIT_BYTES),
        name="output_kernel",
    )(x1, o, wg, woutb, final_norm.reshape(1, D_MODEL).astype(F32))
    return out
```

```python
import math

import jax
import jax.numpy as jnp
from jax import lax
from jax.experimental import pallas as pl
from jax.experimental.pallas import tpu as pltpu

D_MODEL = 1024
E_A = D_MODEL
CONV_WIDTH = 3
N_HEADS = 8
QK_NOPE = 64
QK_ROPE = 32
V_HEAD = 64
KV_RANK = 256
Q_RANK = 384
E_B = N_HEADS * V_HEAD
ROPE_THETA = 10000.0
SOFTMAX_SCALE = 1.0 / math.sqrt(QK_NOPE + QK_ROPE)
EPS = 1e-6

LANES = 128
SUBLANES = 8
BF16_SUBLANES = 2 * SUBLANES
MXU_TILE = 256
V7X_VMEM_BYTES = 64 * 1024 * 1024
HEAD_PACK = LANES
V_PACK = V_HEAD + BF16_SUBLANES
TOKEN_TILE = 512
TOKEN_SUBTILES = 2
OUTPUT_TILE = 2048
OUTPUT_IN_BUFFERS = 3
OUTPUT_ROW_CHUNK = 256
CONV_CHUNK = 128
N_CONV_CHUNKS = E_A // CONV_CHUNK
CARRY_ROWS = SUBLANES
ATT_BLOCK = 512
ATT_QCHUNK = MXU_TILE
HEADS_PER_STEP = 4
ATT_KV_UNROLL = 2
QK_LOOKAHEAD = 2
MASK_VALUE = float(jnp.finfo(jnp.float32).min)
VMEM_LIMIT_BYTES = V7X_VMEM_BYTES * 7 // 8

BF16 = jnp.bfloat16
F32 = jnp.float32


def _rms_scale(x):
    return lax.rsqrt(jnp.mean(x * x, axis=-1, keepdims=True) + EPS)


def _dot(a, b):
    return jnp.dot(a, b, preferred_element_type=F32)


_NT_DIMS = (((1,), (1,)), ((), ()))


def _token_kernel(x_ref, pos_ref, inv_freq_ref, win_ref, conv_ref, wouta_ref,
                  wdkv_ref, wkn_ref, wvt_ref, wcq_ref, wq_ref,
                  x1_ref, q_ref, k_ref, vt_ref,
                  carry_ref, y_ref):
    @pl.when(pl.program_id(1) == 0)
    def _():
        carry_ref[...] = jnp.zeros((CARRY_ROWS, E_A), F32)

    xn = [_conv_mixer_subtile(sub, x_ref, win_ref, conv_ref, wouta_ref, x1_ref, carry_ref, y_ref)
          for sub in range(TOKEN_SUBTILES)]
    _projections(xn, pos_ref, inv_freq_ref, wdkv_ref, wkn_ref, wvt_ref, wcq_ref, wq_ref, q_ref, k_ref, vt_ref)


def _conv_mixer_subtile(sub, x_ref, win_ref, conv_ref, wouta_ref, x1_ref, carry_ref, y_ref):
    tm = TOKEN_TILE
    cw = CONV_CHUNK
    rows = slice(sub * tm, (sub + 1) * tm)
    x = x_ref[rows, :]
    h = (x * _rms_scale(x)).astype(BF16)

    for j in range(N_CONV_CHUNKS):
        cols = slice(j * cw, (j + 1) * cw)
        pj = _dot(h, win_ref[:, j * 4 * cw:(j + 1) * 4 * cw])
        b = pj[:, 0:cw]
        c = pj[:, cw:2 * cw]
        u = pj[:, 2 * cw:3 * cw]
        g = pj[:, 3 * cw:4 * cw]
        v = c * u
        vv = jnp.concatenate([carry_ref[:, cols], v], axis=0)
        carry_ref[:, cols] = v[tm - CARRY_ROWS:tm]
        conv = None
        for k in range(CONV_WIDTH):
            shift = CONV_WIDTH - 1 - k
            tap = conv_ref[k:k + 1, cols] * (pltpu.roll(vv, shift, axis=0)[CARRY_ROWS:] if shift else v)
            conv = tap if conv is None else conv + tap
        y = (g * jax.nn.sigmoid(g)) * b * conv
        y_ref[rows, cols] = y.astype(BF16)

    x1 = x + _dot(y_ref[rows, :], wouta_ref[...])
    x1_ref[rows, :] = x1
    return (x1 * _rms_scale(x1)).astype(BF16)


def _projections(xn, pos_ref, inv_freq_ref, wdkv_ref, wkn_ref, wvt_ref, wcq_ref, wq_ref, q_ref, k_ref, vt_ref):
    tm = TOKEN_TILE
    subs = range(len(xn))
    rows = [slice(s * tm, (s + 1) * tm) for s in subs]
    lane = lax.broadcasted_iota(jnp.int32, (1, LANES), 1)
    low_half = lane < LANES // 2

    reps = LANES // QK_ROPE
    cc, ss = [], []
    for r in rows:
        ang = inv_freq_ref[...] * pos_ref[:, r].astype(F32)
        cos, sin = jnp.cos(ang), jnp.sin(ang)
        cc.append(jnp.concatenate([cos, cos] * reps, axis=0).T)
        ss.append(jnp.concatenate([-sin, sin] * reps, axis=0).T)

    ckr = [_dot(xn[s], wdkv_ref[...]) for s in subs]
    cq = [_dot(xn[s], wcq_ref[...]) for s in subs]
    rope_terms = [ckr[s][:, KV_RANK:KV_RANK + LANES] * jnp.where(low_half, cc[s], ss[s]) for s in subs]
    rope_k = [t + pltpu.roll(t, LANES // 2, axis=1) for t in rope_terms]
    ckvn = [(ckr[s][:, 0:KV_RANK] * _rms_scale(ckr[s][:, 0:KV_RANK])).astype(BF16) for s in subs]
    cqn = [(c * _rms_scale(c)).astype(BF16) for c in cq]
    kn = [_dot(c, wkn_ref[...]) for c in ckvn]
    q = [_dot(c, wq_ref[...]) for c in cqn]
    head_row = lax.broadcasted_iota(jnp.int32, (N_HEADS * V_PACK, 1), 0) % V_PACK
    ones_row = jnp.where(head_row == V_HEAD, 1.0, 0.0)
    for s in subs:
        vt = lax.dot_general(wvt_ref[...], ckvn[s], _NT_DIMS, preferred_element_type=F32)
        vt_ref[s] = (vt + ones_row).astype(BF16)

    qscale = SOFTMAX_SCALE * math.log2(math.e)
    for s in subs:
        tq = jnp.where(lane < QK_NOPE, 1.0, jnp.where(lane < QK_NOPE + QK_ROPE, cc[s], ss[s])) * qscale
        for hd in range(N_HEADS):
            hc = slice(hd * HEAD_PACK, (hd + 1) * HEAD_PACK)
            kn_pair = kn[s][:, (hd // 2) * LANES:(hd // 2 + 1) * LANES]
            if hd % 2 == 1:
                kn_pair = pltpu.roll(kn_pair, LANES // 2, axis=1)
            k_ref[rows[s], hc] = jnp.where(low_half, kn_pair, rope_k[s]).astype(BF16)
            q_ref[rows[s], hc] = (q[s][:, hc] * tq).astype(BF16)


def _attention_kernel(q_ref, k_ref, vt_ref, o_ref, m_ref, alpha_ref, acc_ref, *st_refs):
    blk = ATT_BLOCK
    cw = ATT_QCHUNK
    nblk = q_ref.shape[0] // blk
    chains = [(hd, c) for hd in range(HEADS_PER_STEP) for c in range(blk // cw)]
    assert len(st_refs) == len(chains)

    def head_cols(i):
        hd = chains[i][0]
        return slice(hd * HEAD_PACK, (hd + 1) * HEAD_PACK)

    def state(ref, i):
        hd, c = chains[i]
        return ref.at[hd, :, c * cw:(c + 1) * cw]

    def qk(i, qi, ki, diagonal):
        c = chains[i][1]
        rows = (c + 1) * cw if diagonal else blk
        q = q_ref[pl.ds(pl.multiple_of(qi * blk + c * cw, cw), cw), head_cols(i)]
        k = k_ref[pl.ds(pl.multiple_of(ki * blk, blk), rows), head_cols(i)]
        return lax.dot_general(k, q, _NT_DIMS, preferred_element_type=F32)

    def stage_scores(i, s, diagonal):
        st = st_refs[i]
        if diagonal:
            c = chains[i][1]
            lo, hi = c * cw, (c + 1) * cw
            kv_idx = lax.broadcasted_iota(jnp.int32, (cw, cw), 0)
            q_idx = lax.broadcasted_iota(jnp.int32, (cw, cw), 1)
            tri = jnp.where(kv_idx <= q_idx, s[lo:hi], MASK_VALUE)
            m_new = jnp.max(tri, axis=0, keepdims=True)
            st[lo:hi, :] = tri
            if lo > 0:
                m_new = jnp.maximum(m_new, jnp.max(s[0:lo], axis=0, keepdims=True))
                st[0:lo, :] = s[0:lo]
            if hi < blk:
                st[hi:blk, :] = jnp.full((blk - hi, cw), MASK_VALUE, F32)
            m_old = jnp.full((1, cw), MASK_VALUE, F32)
        else:
            m_old = state(m_ref, i)[...]
            m_new = jnp.maximum(m_old, jnp.max(s, axis=0, keepdims=True))
            st[...] = s
        state(alpha_ref, i)[...] = jnp.exp2(m_old - m_new)
        state(m_ref, i)[...] = m_new

    def accumulate(i, ki, staged_diagonal=False):
        hd, c = chains[i]
        rows = (c + 1) * cw if staged_diagonal else blk
        vt = vt_ref[ki, hd * V_PACK:(hd + 1) * V_PACK, 0:rows]
        p = jnp.exp2(st_refs[i][0:rows, :] - state(m_ref, i)[...]).astype(BF16)
        acc = state(acc_ref, i)
        acc[...] = state(alpha_ref, i)[...] * acc[...] + _dot(vt, p)

    def step(qi, ki, diagonal, staged_ki, staged_diagonal=False):
        s = {j: qk(j, qi, ki, diagonal) for j in range(QK_LOOKAHEAD)}
        for i in range(len(chains)):
            if i + QK_LOOKAHEAD < len(chains):
                s[i + QK_LOOKAHEAD] = qk(i + QK_LOOKAHEAD, qi, ki, diagonal)
            accumulate(i, staged_ki, staged_diagonal)
            stage_scores(i, s.pop(i), diagonal)

    def finalize(qi):
        o_t = jnp.concatenate([acc_ref[hd, 0:V_HEAD, :] / acc_ref[hd, V_HEAD:V_HEAD + 1, :]
                               for hd in range(HEADS_PER_STEP)], axis=0)
        o_ref[pl.ds(pl.multiple_of(qi * blk, blk), blk), :] = o_t.T.astype(o_ref.dtype)

    acc_ref[...] = jnp.zeros(acc_ref.shape, F32)
    for i in range(len(chains)):
        stage_scores(i, qk(i, 0, 0, True), True)

    def q_block(qi, staged_ki):
        step(qi, qi, True, staged_ki)
        finalize(qi - 1)
        step(qi, 0, False, qi, staged_diagonal=True)

        rest = qi - 1
        unroll = ATT_KV_UNROLL

        def kv_group(j, staged):
            for u in range(unroll):
                ki = unroll * j + u + 1
                step(qi, ki, False, staged)
                staged = ki
            return staged

        n_groups = lax.shift_right_logical(rest, unroll.bit_length() - 1)
        staged = lax.fori_loop(0, n_groups, kv_group, 0)

        def kv_single(ki, staged):
            step(qi, ki, False, staged)
            return ki

        return lax.fori_loop(n_groups * unroll + 1, qi, kv_single, staged)

    staged_ki = lax.fori_loop(1, nblk, q_block, 0)
    for i in range(len(chains)):
        accumulate(i, staged_ki)
    finalize(nblk - 1)


def _output_kernel(x1_ref, o_ref, wg_ref, woutb_ref, fnorm_ref, out_ref):
    rc = OUTPUT_ROW_CHUNK
    n_chunks = x1_ref.shape[0] // rc
    rows = lambda c: slice(c * rc, (c + 1) * rc)
    hb, g, y = {}, {}, {}

    def norm_in(c):
        x1 = x1_ref[rows(c), :]
        hb[c] = (x1 * _rms_scale(x1)).astype(BF16)

    def gate_proj(c):
        g[c] = _dot(hb.pop(c), wg_ref[...])

    def gate(c):
        gc = g.pop(c)
        y[c] = (o_ref[rows(c), :].astype(F32) * (gc * jax.nn.sigmoid(gc))).astype(BF16)

    def out_proj_norm(c):
        x2 = x1_ref[rows(c), :] + _dot(y.pop(c), woutb_ref[...])
        out_ref[rows(c), :] = x2 * _rms_scale(x2) * fnorm_ref[...]

    stages = (norm_in, gate_proj, gate, out_proj_norm)
    for t in range(n_chunks + len(stages) - 1):
        for depth, stage in reversed(list(enumerate(stages))):
            if 0 <= t - depth < n_chunks:
                stage(t - depth)


def _const_spec(shape):
    return pl.BlockSpec(shape, lambda *_: (0,) * len(shape))


def _swap_halves(w):
    half = w.shape[-1] // 2
    return jnp.concatenate([w[..., half:], w[..., :half]], axis=-1)


def _prepare_weights(a_norm, a_w_in, kv_norm, w_dkv, ckv_norm, w_ukv, b_norm, b_w_in, q_norm, b_w_uq):
    rows = lambda g: g.astype(F32)[:, None]

    a_w_in = rows(a_norm) * a_w_in
    win = jnp.concatenate(
        [a_w_in[:, part * E_A + j * CONV_CHUNK:part * E_A + (j + 1) * CONV_CHUNK]
         for j in range(N_CONV_CHUNKS) for part in range(4)], axis=-1).astype(BF16)

    w_dkv = rows(kv_norm) * w_dkv
    kr = w_dkv[:, KV_RANK:]
    wdkv = jnp.concatenate([w_dkv[:, :KV_RANK], kr, kr, _swap_halves(kr), _swap_halves(kr)],
                           axis=-1).astype(BF16)

    wk = (rows(ckv_norm) * w_ukv).reshape(KV_RANK, N_HEADS, QK_NOPE + V_HEAD)
    wkn = wk[..., :QK_NOPE].reshape(KV_RANK, N_HEADS * QK_NOPE).astype(BF16)
    pad_v = jnp.zeros((KV_RANK, N_HEADS, V_PACK - V_HEAD), F32)
    wvt = jnp.concatenate([wk[..., QK_NOPE:], pad_v], axis=-1).reshape(KV_RANK, N_HEADS * V_PACK)
    wvt = wvt.T.astype(BF16)

    b_w_in = rows(b_norm) * b_w_in
    wcq = b_w_in[:, :Q_RANK].astype(BF16)
    wg = b_w_in[:, Q_RANK:].astype(BF16)

    wq = (rows(q_norm) * b_w_uq).reshape(Q_RANK, N_HEADS, QK_NOPE + QK_ROPE)
    rope = wq[..., QK_NOPE:]
    wq = jnp.concatenate([wq[..., :QK_NOPE], rope, _swap_halves(rope)], axis=-1)
    wq = wq.reshape(Q_RANK, N_HEADS * HEAD_PACK).astype(BF16)
    return win, wdkv, wkn, wvt, wcq, wg, wq


def kernel(x, positions, a_norm, a_w_in, a_conv, a_w_out, kv_norm, w_dkv, ckv_norm, w_ukv,
           b_norm, b_w_in, b_q_norm, b_w_uq, b_w_out, final_norm):
    bsz, seq, d = x.shape
    assert d == D_MODEL and seq % (TOKEN_SUBTILES * TOKEN_TILE) == 0
    assert seq % ATT_BLOCK == 0 and seq % OUTPUT_TILE == 0
    assert a_norm.shape[0] == 1 and b_norm.shape[0] == 1, "depth-2 trunk: one conv layer, one MLA layer"
    assert a_conv.shape[1:] == (CONV_WIDTH, E_A) and CONV_WIDTH - 1 <= CARRY_ROWS

    inv_freq = ROPE_THETA ** (-jnp.arange(0, QK_ROPE, 2, dtype=F32) / QK_ROPE)
    inv_freq = inv_freq.reshape(QK_ROPE // 2, 1)
    pos = positions.reshape(bsz, 1, seq)
    win, wdkv, wkn, wvt, wcq, wg, wq = _prepare_weights(
        a_norm[0], a_w_in[0], kv_norm, w_dkv, ckv_norm, w_ukv, b_norm[0], b_w_in[0], b_q_norm[0], b_w_uq[0])
    wouta = a_w_out[0].astype(BF16)
    woutb = b_w_out[0].astype(BF16)

    tm = TOKEN_TILE
    assert tm == ATT_BLOCK, "v'^T is written one attention kv block per token sub-tile"
    n_tiles = seq // tm
    step_rows = TOKEN_SUBTILES * tm
    hp = N_HEADS * HEAD_PACK
    tok_spec = lambda width: pl.BlockSpec((None, step_rows, width), lambda b, s: (b, s, 0))
    pos_spec = pl.BlockSpec((None, 1, step_rows), lambda b, s: (b, 0, s))

    x1, q, k, vt = pl.pallas_call(
        _token_kernel,
        grid=(bsz, seq // step_rows),
        in_specs=[
            tok_spec(D_MODEL), pos_spec, _const_spec(inv_freq.shape),
            _const_spec(win.shape), _const_spec(a_conv.shape[1:]), _const_spec(wouta.shape),
            _const_spec(wdkv.shape), _const_spec(wkn.shape), _const_spec(wvt.shape),
            _const_spec(wcq.shape), _const_spec(wq.shape),
        ],
        out_specs=[tok_spec(D_MODEL), tok_spec(hp), tok_spec(hp),
                   pl.BlockSpec((None, TOKEN_SUBTILES, N_HEADS * V_PACK, tm),
                                lambda b, s: (b, s, 0, 0))],
        out_shape=[
            jax.ShapeDtypeStruct((bsz, seq, D_MODEL), F32),
            jax.ShapeDtypeStruct((bsz, seq, hp), BF16),
            jax.ShapeDtypeStruct((bsz, seq, hp), BF16),
            jax.ShapeDtypeStruct((bsz, n_tiles, N_HEADS * V_PACK, tm), BF16),
        ],
        scratch_shapes=[
            pltpu.VMEM((CARRY_ROWS, E_A), F32),
            pltpu.VMEM((step_rows, E_A), BF16),
        ],
        compiler_params=pltpu.CompilerParams(
            dimension_semantics=("arbitrary", "arbitrary"),
            vmem_limit_bytes=VMEM_LIMIT_BYTES),
        name="token_kernel",
    )(x, pos, inv_freq, win, a_conv[0], wouta, wdkv, wkn, wvt, wcq, wq)

    hw = HEADS_PER_STEP * HEAD_PACK
    att_spec = pl.BlockSpec((None, seq, hw), lambda b, h: (b, 0, h))
    o = pl.pallas_call(
        _attention_kernel,
        grid=(bsz, N_HEADS // HEADS_PER_STEP),
        in_specs=[att_spec, att_spec,
                  pl.BlockSpec((None, n_tiles, HEADS_PER_STEP * V_PACK, ATT_BLOCK),
                               lambda b, h: (b, 0, h, 0))],
        out_specs=pl.BlockSpec((None, seq, HEADS_PER_STEP * V_HEAD), lambda b, h: (b, 0, h)),
        out_shape=jax.ShapeDtypeStruct((bsz, seq, E_B), BF16),
        scratch_shapes=[
            pltpu.VMEM((HEADS_PER_STEP, 1, ATT_BLOCK), F32),
            pltpu.VMEM((HEADS_PER_STEP, 1, ATT_BLOCK), F32),
            pltpu.VMEM((HEADS_PER_STEP, V_PACK, ATT_BLOCK), F32),
        ] + [pltpu.VMEM((ATT_BLOCK, ATT_QCHUNK), F32)
             for _ in range(HEADS_PER_STEP * (ATT_BLOCK // ATT_QCHUNK))],
        compiler_params=pltpu.CompilerParams(
            dimension_semantics=("arbitrary", "arbitrary"),
            vmem_limit_bytes=VMEM_LIMIT_BYTES),
        name="attention_kernel",
    )(q, k, vt)

    n_rows = bsz * seq
    row_spec = lambda width, **kw: pl.BlockSpec((OUTPUT_TILE, width), lambda i: (i, 0), **kw)

    def output_pipeline(x1_hbm, o_hbm, wg_ref, woutb_ref, fnorm_ref, out_hbm):
        pltpu.emit_pipeline(
            lambda x1_ref, o_ref, out_ref: _output_kernel(x1_ref, o_ref, wg_ref, woutb_ref, fnorm_ref, out_ref),
            grid=(n_rows // OUTPUT_TILE,),
            in_specs=[row_spec(D_MODEL, pipeline_mode=pl.Buffered(OUTPUT_IN_BUFFERS)),
                      row_spec(E_B, pipeline_mode=pl.Buffered(OUTPUT_IN_BUFFERS))],
            out_specs=[row_spec(D_MODEL)],
        )(x1_hbm, o_hbm, out_hbm)

    any_spec = pl.BlockSpec(memory_space=pl.ANY)
    vmem_spec = pl.BlockSpec(memory_space=pltpu.VMEM)
    out = pl.pallas_call(
        output_pipeline,
        in_specs=[any_spec, any_spec, vmem_spec, vmem_spec, vmem_spec],
        out_specs=any_spec,
        out_shape=jax.ShapeDtypeStruct((n_rows, D_MODEL), F32),
        compiler_params=pltpu.CompilerParams(vmem_limit_bytes=VMEM_LIMIT_BYTES),
        name="output_kernel",
    )(x1.reshape(n_rows, D_MODEL), o.reshape(n_rows, E_B), wg, woutb,
      final_norm.reshape(1, D_MODEL).astype(F32))
    return out.reshape(bsz, seq, D_MODEL)
```

```python
import math
from functools import partial

import jax
import jax.numpy as jnp
from jax import lax
from jax.experimental import pallas as pl
from jax.experimental.pallas import tpu as pltpu

D_MODEL = 1024
E_A = D_MODEL
CONV_WIDTH = 3
N_HEADS = 8
QK_NOPE = 64
QK_ROPE = 32
V_HEAD = 64
KV_RANK = 256
Q_RANK = 384
E_B = N_HEADS * V_HEAD
ROPE_THETA = 10000.0
SOFTMAX_SCALE = 1.0 / math.sqrt(QK_NOPE + QK_ROPE)
EPS = 1e-6

LANES = 128
SUBLANES = 8
BF16_SUBLANES = 2 * SUBLANES
MXU_TILE = 256
V7X_VMEM_BYTES = 64 * 1024 * 1024
HEAD_PACK = LANES
V_PACK = V_HEAD + BF16_SUBLANES
TOKEN_TILE = 512
TOKEN_SUBTILES = 2
OUTPUT_TILE = 2048
OUTPUT_ROW_CHUNK = 256
CONV_CHUNK = 128
N_CONV_CHUNKS = E_A // CONV_CHUNK
CARRY_ROWS = SUBLANES
ATT_BLOCK = 512
ATT_QCHUNK = MXU_TILE
HEADS_PER_STEP = 4
ATT_KV_UNROLL = 2
QK_LOOKAHEAD = 2
MASK_VALUE = float(jnp.finfo(jnp.float32).min)
VMEM_LIMIT_BYTES = V7X_VMEM_BYTES * 7 // 8

BF16 = jnp.bfloat16
F32 = jnp.float32


def _rms_scale(x):
    return lax.rsqrt(jnp.mean(x * x, axis=-1, keepdims=True) + EPS)


def _dot(a, b):
    return jnp.dot(a, b, preferred_element_type=F32)


_NT_DIMS = (((1,), (1,)), ((), ()))


def _token_kernel(x_ref, pos_ref, inv_freq_ref, win_ref, conv_ref, wouta_ref,
                  wdkv_ref, wkn_ref, wvt_ref, wcq_ref, wq_ref,
                  x1_ref, q_ref, k_ref, vt_ref,
                  carry_ref, y_ref):
    @pl.when(pl.program_id(1) == 0)
    def _():
        carry_ref[...] = jnp.zeros((CARRY_ROWS, E_A), F32)

    xn = [_conv_mixer_subtile(sub, x_ref, win_ref, conv_ref, wouta_ref, x1_ref, carry_ref, y_ref)
          for sub in range(TOKEN_SUBTILES)]
    _projections(xn, pos_ref, inv_freq_ref, wdkv_ref, wkn_ref, wvt_ref, wcq_ref, wq_ref, q_ref, k_ref, vt_ref)


def _conv_mixer_subtile(sub, x_ref, win_ref, conv_ref, wouta_ref, x1_ref, carry_ref, y_ref):
    tm = TOKEN_TILE
    cw = CONV_CHUNK
    rows = slice(sub * tm, (sub + 1) * tm)
    x = x_ref[rows, :]
    h = (x * _rms_scale(x)).astype(BF16)

    for j in range(N_CONV_CHUNKS):
        cols = slice(j * cw, (j + 1) * cw)
        pj = _dot(h, win_ref[:, j * 4 * cw:(j + 1) * 4 * cw])
        b = pj[:, 0:cw]
        c = pj[:, cw:2 * cw]
        u = pj[:, 2 * cw:3 * cw]
        g = pj[:, 3 * cw:4 * cw]
        v = c * u
        vv = jnp.concatenate([carry_ref[:, cols], v], axis=0)
        carry_ref[:, cols] = v[tm - CARRY_ROWS:tm]
        conv = None
        for k in range(CONV_WIDTH):
            shift = CONV_WIDTH - 1 - k
            tap = conv_ref[k:k + 1, cols] * (pltpu.roll(vv, shift, axis=0)[CARRY_ROWS:] if shift else v)
            conv = tap if conv is None else conv + tap
        y = (g * jax.nn.sigmoid(g)) * b * conv
        y_ref[rows, cols] = y.astype(BF16)

    x1 = x + _dot(y_ref[rows, :], wouta_ref[...])
    x1_ref[rows, :] = x1
    return (x1 * _rms_scale(x1)).astype(BF16)


def _projections(xn, pos_ref, inv_freq_ref, wdkv_ref, wkn_ref, wvt_ref, wcq_ref, wq_ref, q_ref, k_ref, vt_ref):
    tm = TOKEN_TILE
    subs = range(len(xn))
    rows = [slice(s * tm, (s + 1) * tm) for s in subs]
    lane = lax.broadcasted_iota(jnp.int32, (1, LANES), 1)
    low_half = lane < LANES // 2

    reps = LANES // QK_ROPE
    cc, ss = [], []
    for r in rows:
        ang = inv_freq_ref[...] * pos_ref[:, r].astype(F32)
        cos, sin = jnp.cos(ang), jnp.sin(ang)
        cc.append(jnp.concatenate([cos, cos] * reps, axis=0).T)
        ss.append(jnp.concatenate([-sin, sin] * reps, axis=0).T)

    ckr = [_dot(xn[s], wdkv_ref[...]) for s in subs]
    cq = [_dot(xn[s], wcq_ref[...]) for s in subs]
    rope_terms = [ckr[s][:, KV_RANK:KV_RANK + LANES] * jnp.where(low_half, cc[s], ss[s]) for s in subs]
    rope_k = [t + pltpu.roll(t, LANES // 2, axis=1) for t in rope_terms]
    ckvn = [(ckr[s][:, 0:KV_RANK] * _rms_scale(ckr[s][:, 0:KV_RANK])).astype(BF16) for s in subs]
    cqn = [(c * _rms_scale(c)).astype(BF16) for c in cq]
    kn = [_dot(c, wkn_ref[...]) for c in ckvn]
    q = [_dot(c, wq_ref[...]) for c in cqn]
    head_row = lax.broadcasted_iota(jnp.int32, (N_HEADS * V_PACK, 1), 0) % V_PACK
    ones_row = jnp.where(head_row == V_HEAD, 1.0, 0.0)
    for s in subs:
        vt = lax.dot_general(wvt_ref[...], ckvn[s], _NT_DIMS, preferred_element_type=F32)
        vt_ref[s] = (vt + ones_row).astype(BF16)

    qscale = SOFTMAX_SCALE * math.log2(math.e)
    for s in subs:
        tq = jnp.where(lane < QK_NOPE, 1.0, jnp.where(lane < QK_NOPE + QK_ROPE, cc[s], ss[s])) * qscale
        for hd in range(N_HEADS):
            hc = slice(hd * HEAD_PACK, (hd + 1) * HEAD_PACK)
            kn_pair = kn[s][:, (hd // 2) * LANES:(hd // 2 + 1) * LANES]
            if hd % 2 == 1:
                kn_pair = pltpu.roll(kn_pair, LANES // 2, axis=1)
            k_ref[rows[s], hc] = jnp.where(low_half, kn_pair, rope_k[s]).astype(BF16)
            q_ref[rows[s], hc] = (q[s][:, hc] * tq).astype(BF16)


def _attention_kernel(q_ref, k_ref, vt_ref, o_ref, m_ref, alpha_ref, acc_ref, *st_refs):
    blk = ATT_BLOCK
    cw = ATT_QCHUNK
    nblk = q_ref.shape[0] // blk
    chains = [(hd, c) for hd in range(HEADS_PER_STEP) for c in range(blk // cw)]
    assert len(st_refs) == len(chains)

    def head_cols(i):
        hd = chains[i][0]
        return slice(hd * HEAD_PACK, (hd + 1) * HEAD_PACK)

    def state(ref, i):
        hd, c = chains[i]
        return ref.at[hd, :, c * cw:(c + 1) * cw]

    def qk(i, qi, ki, diagonal):
        c = chains[i][1]
        rows = (c + 1) * cw if diagonal else blk
        q = q_ref[pl.ds(pl.multiple_of(qi * blk + c * cw, cw), cw), head_cols(i)]
        k = k_ref[pl.ds(pl.multiple_of(ki * blk, blk), rows), head_cols(i)]
        return lax.dot_general(k, q, _NT_DIMS, preferred_element_type=F32)

    def stage_scores(i, s, diagonal):
        st = st_refs[i]
        if diagonal:
            c = chains[i][1]
            lo, hi = c * cw, (c + 1) * cw
            kv_idx = lax.broadcasted_iota(jnp.int32, (cw, cw), 0)
            q_idx = lax.broadcasted_iota(jnp.int32, (cw, cw), 1)
            tri = jnp.where(kv_idx <= q_idx, s[lo:hi], MASK_VALUE)
            m_new = jnp.max(tri, axis=0, keepdims=True)
            st[lo:hi, :] = tri
            if lo > 0:
                m_new = jnp.maximum(m_new, jnp.max(s[0:lo], axis=0, keepdims=True))
                st[0:lo, :] = s[0:lo]
            if hi < blk:
                st[hi:blk, :] = jnp.full((blk - hi, cw), MASK_VALUE, F32)
            m_old = jnp.full((1, cw), MASK_VALUE, F32)
        else:
            m_old = state(m_ref, i)[...]
            m_new = jnp.maximum(m_old, jnp.max(s, axis=0, keepdims=True))
            st[...] = s
        state(alpha_ref, i)[...] = jnp.exp2(m_old - m_new)
        state(m_ref, i)[...] = m_new

    def accumulate(i, ki, staged_diagonal=False):
        hd, c = chains[i]
        rows = (c + 1) * cw if staged_diagonal else blk
        vt = vt_ref[ki, hd * V_PACK:(hd + 1) * V_PACK, 0:rows]
        p = jnp.exp2(st_refs[i][0:rows, :] - state(m_ref, i)[...]).astype(BF16)
        acc = state(acc_ref, i)
        acc[...] = state(alpha_ref, i)[...] * acc[...] + _dot(vt, p)

    def step(qi, ki, diagonal, staged_ki, staged_diagonal=False):
        s = {j: qk(j, qi, ki, diagonal) for j in range(QK_LOOKAHEAD)}
        for i in range(len(chains)):
            if i + QK_LOOKAHEAD < len(chains):
                s[i + QK_LOOKAHEAD] = qk(i + QK_LOOKAHEAD, qi, ki, diagonal)
            accumulate(i, staged_ki, staged_diagonal)
            stage_scores(i, s.pop(i), diagonal)

    def finalize(qi):
        o_t = jnp.concatenate([acc_ref[hd, 0:V_HEAD, :] / acc_ref[hd, V_HEAD:V_HEAD + 1, :]
                               for hd in range(HEADS_PER_STEP)], axis=0)
        o_ref[pl.ds(pl.multiple_of(qi * blk, blk), blk), :] = o_t.T.astype(o_ref.dtype)

    acc_ref[...] = jnp.zeros(acc_ref.shape, F32)
    for i in range(len(chains)):
        stage_scores(i, qk(i, 0, 0, True), True)

    def q_block(qi, staged_ki, fused):
        step(qi, qi, True, staged_ki)
        finalize(qi - 1)
        step(qi, 0, False, qi, staged_diagonal=True)
        for ki in range(1, fused):
            step(qi, ki, False, ki - 1)

        unroll = ATT_KV_UNROLL

        def kv_group(j, staged):
            for u in range(unroll):
                ki = unroll * j + u + fused
                step(qi, ki, False, staged)
                staged = ki
            return staged

        n_groups = lax.shift_right_logical(qi - fused, unroll.bit_length() - 1)
        staged = lax.fori_loop(0, n_groups, kv_group, fused - 1)

        def kv_single(ki, staged):
            step(qi, ki, False, staged)
            return ki

        return lax.fori_loop(n_groups * unroll + fused, qi, kv_single, staged)

    staged_ki = q_block(jnp.int32(1), 0, 1)
    staged_ki = lax.fori_loop(2, nblk, partial(q_block, fused=2), staged_ki)
    for i in range(len(chains)):
        accumulate(i, staged_ki)
    finalize(nblk - 1)


def _output_kernel(x1_ref, o_ref, wg_ref, woutb_ref, fnorm_ref, out_ref):
    rc = OUTPUT_ROW_CHUNK
    n_chunks = x1_ref.shape[0] // rc
    rows = lambda c: slice(c * rc, (c + 1) * rc)
    hb, g, y = {}, {}, {}

    def norm_in(c):
        x1 = x1_ref[rows(c), :]
        hb[c] = (x1 * _rms_scale(x1)).astype(BF16)

    def gate_proj(c):
        g[c] = _dot(hb.pop(c), wg_ref[...])

    def gate(c):
        gc = g.pop(c)
        y[c] = (o_ref[rows(c), :].astype(F32) * (gc * jax.nn.sigmoid(gc))).astype(BF16)

    def out_proj_norm(c):
        x2 = x1_ref[rows(c), :] + _dot(y.pop(c), woutb_ref[...])
        out_ref[rows(c), :] = x2 * _rms_scale(x2) * fnorm_ref[...]

    stages = (norm_in, gate_proj, gate, out_proj_norm)
    for t in range(n_chunks + len(stages) - 1):
        for depth, stage in reversed(list(enumerate(stages))):
            if 0 <= t - depth < n_chunks:
                stage(t - depth)


def _const_spec(shape):
    return pl.BlockSpec(shape, lambda *_: (0,) * len(shape))


def _swap_halves(w):
    half = w.shape[-1] // 2
    return jnp.concatenate([w[..., half:], w[..., :half]], axis=-1)


def _prepare_weights(a_norm, a_w_in, kv_norm, w_dkv, ckv_norm, w_ukv, b_norm, b_w_in, q_norm, b_w_uq):
    rows = lambda g: g.astype(F32)[:, None]

    a_w_in = rows(a_norm) * a_w_in
    win = jnp.concatenate(
        [a_w_in[:, part * E_A + j * CONV_CHUNK:part * E_A + (j + 1) * CONV_CHUNK]
         for j in range(N_CONV_CHUNKS) for part in range(4)], axis=-1).astype(BF16)

    w_dkv = rows(kv_norm) * w_dkv
    kr = w_dkv[:, KV_RANK:]
    wdkv = jnp.concatenate([w_dkv[:, :KV_RANK], kr, kr, _swap_halves(kr), _swap_halves(kr)],
                           axis=-1).astype(BF16)

    wk = (rows(ckv_norm) * w_ukv).reshape(KV_RANK, N_HEADS, QK_NOPE + V_HEAD)
    wkn = wk[..., :QK_NOPE].reshape(KV_RANK, N_HEADS * QK_NOPE).astype(BF16)
    pad_v = jnp.zeros((KV_RANK, N_HEADS, V_PACK - V_HEAD), F32)
    wvt = jnp.concatenate([wk[..., QK_NOPE:], pad_v], axis=-1).reshape(KV_RANK, N_HEADS * V_PACK)
    wvt = wvt.T.astype(BF16)

    b_w_in = rows(b_norm) * b_w_in
    wcq = b_w_in[:, :Q_RANK].astype(BF16)
    wg = b_w_in[:, Q_RANK:].astype(BF16)

    wq = (rows(q_norm) * b_w_uq).reshape(Q_RANK, N_HEADS, QK_NOPE + QK_ROPE)
    rope = wq[..., QK_NOPE:]
    wq = jnp.concatenate([wq[..., :QK_NOPE], rope, _swap_halves(rope)], axis=-1)
    wq = wq.reshape(Q_RANK, N_HEADS * HEAD_PACK).astype(BF16)
    return win, wdkv, wkn, wvt, wcq, wg, wq


def kernel(x, positions, a_norm, a_w_in, a_conv, a_w_out, kv_norm, w_dkv, ckv_norm, w_ukv,
           b_norm, b_w_in, b_q_norm, b_w_uq, b_w_out, final_norm):
    bsz, seq, d = x.shape
    assert d == D_MODEL and seq % (TOKEN_SUBTILES * TOKEN_TILE) == 0
    assert seq % ATT_BLOCK == 0 and seq % OUTPUT_TILE == 0
    assert a_norm.shape[0] == 1 and b_norm.shape[0] == 1, "depth-2 trunk: one conv layer, one MLA layer"
    assert a_conv.shape[1:] == (CONV_WIDTH, E_A) and CONV_WIDTH - 1 <= CARRY_ROWS

    inv_freq = ROPE_THETA ** (-jnp.arange(0, QK_ROPE, 2, dtype=F32) / QK_ROPE)
    inv_freq = inv_freq.reshape(QK_ROPE // 2, 1)
    pos = positions.reshape(bsz, 1, seq)
    win, wdkv, wkn, wvt, wcq, wg, wq = _prepare_weights(
        a_norm[0], a_w_in[0], kv_norm, w_dkv, ckv_norm, w_ukv, b_norm[0], b_w_in[0], b_q_norm[0], b_w_uq[0])
    wouta = a_w_out[0].astype(BF16)
    woutb = b_w_out[0].astype(BF16)

    tm = TOKEN_TILE
    assert tm == ATT_BLOCK, "v'^T is written one attention kv block per token sub-tile"
    n_tiles = seq // tm
    step_rows = TOKEN_SUBTILES * tm
    hp = N_HEADS * HEAD_PACK
    tok_spec = lambda width: pl.BlockSpec((None, step_rows, width), lambda b, s: (b, s, 0))
    pos_spec = pl.BlockSpec((None, 1, step_rows), lambda b, s: (b, 0, s))

    x1, q, k, vt = pl.pallas_call(
        _token_kernel,
        grid=(bsz, seq // step_rows),
        in_specs=[
            tok_spec(D_MODEL), pos_spec, _const_spec(inv_freq.shape),
            _const_spec(win.shape), _const_spec(a_conv.shape[1:]), _const_spec(wouta.shape),
            _const_spec(wdkv.shape), _const_spec(wkn.shape), _const_spec(wvt.shape),
            _const_spec(wcq.shape), _const_spec(wq.shape),
        ],
        out_specs=[tok_spec(D_MODEL), tok_spec(hp), tok_spec(hp),
                   pl.BlockSpec((None, TOKEN_SUBTILES, N_HEADS * V_PACK, tm),
                                lambda b, s: (b, s, 0, 0))],
        out_shape=[
            jax.ShapeDtypeStruct((bsz, seq, D_MODEL), F32),
            jax.ShapeDtypeStruct((bsz, seq, hp), BF16),
            jax.ShapeDtypeStruct((bsz, seq, hp), BF16),
            jax.ShapeDtypeStruct((bsz, n_tiles, N_HEADS * V_PACK, tm), BF16),
        ],
        scratch_shapes=[
            pltpu.VMEM((CARRY_ROWS, E_A), F32),
            pltpu.VMEM((step_rows, E_A), BF16),
        ],
        compiler_params=pltpu.CompilerParams(
            dimension_semantics=("arbitrary", "arbitrary"),
            vmem_limit_bytes=VMEM_LIMIT_BYTES),
        name="token_kernel",
    )(x, pos, inv_freq, win, a_conv[0], wouta, wdkv, wkn, wvt, wcq, wq)

    hw = HEADS_PER_STEP * HEAD_PACK
    att_spec = pl.BlockSpec((None, seq, hw), lambda b, h: (b, 0, h))
    o = pl.pallas_call(
        _attention_kernel,
        grid=(bsz, N_HEADS // HEADS_PER_STEP),
        in_specs=[att_spec, att_spec,
                  pl.BlockSpec((None, n_tiles, HEADS_PER_STEP * V_PACK, ATT_BLOCK),
                               lambda b, h: (b, 0, h, 0))],
        out_specs=pl.BlockSpec((None, seq, HEADS_PER_STEP * V_HEAD), lambda b, h: (b, 0, h)),
        out_shape=jax.ShapeDtypeStruct((bsz, seq, E_B), BF16),
        scratch_shapes=[
            pltpu.VMEM((HEADS_PER_STEP, 1, ATT_BLOCK), F32),
            pltpu.VMEM((HEADS_PER_STEP, 1, ATT_BLOCK), F32),
            pltpu.VMEM((HEADS_PER_STEP, V_PACK, ATT_BLOCK), F32),
        ] + [pltpu.VMEM((ATT_BLOCK, ATT_QCHUNK), F32)
             for _ in range(HEADS_PER_STEP * (ATT_BLOCK // ATT_QCHUNK))],
        compiler_params=pltpu.CompilerParams(
            dimension_semantics=("arbitrary", "arbitrary"),
            vmem_limit_bytes=VMEM_LIMIT_BYTES),
        name="attention_kernel",
    )(q, k, vt)

    out_spec = lambda width: pl.BlockSpec((None, OUTPUT_TILE, width), lambda b, s: (b, s, 0))
    out = pl.pallas_call(
        _output_kernel,
        grid=(bsz, seq // OUTPUT_TILE),
        in_specs=[
            out_spec(D_MODEL), out_spec(E_B), _const_spec(wg.shape), _const_spec(woutb.shape),
            _const_spec((1, D_MODEL)),
        ],
        out_specs=out_spec(D_MODEL),
        out_shape=jax.ShapeDtypeStruct((bsz, seq, D_MODEL), F32),
        compiler_params=pltpu.CompilerParams(
            dimension_semantics=("arbitrary", "arbitrary"),
            vmem_limit_bytes=VMEM_LIMIT_BYTES),
        name="output_kernel",
    )(x1, o, wg, woutb, final_norm.reshape(1, D_MODEL).astype(F32))
    return out
```

```python
import math
from functools import partial

import jax
import jax.numpy as jnp
from jax import lax
from jax.experimental import pallas as pl
from jax.experimental.pallas import tpu as pltpu

D_MODEL = 1024
E_A = D_MODEL
CONV_WIDTH = 3
N_HEADS = 8
QK_NOPE = 64
QK_ROPE = 32
V_HEAD = 64
KV_RANK = 256
Q_RANK = 384
E_B = N_HEADS * V_HEAD
ROPE_THETA = 10000.0
SOFTMAX_SCALE = 1.0 / math.sqrt(QK_NOPE + QK_ROPE)
EPS = 1e-6

LANES = 128
SUBLANES = 8
BF16_SUBLANES = 2 * SUBLANES
MXU_TILE = 256
V7X_VMEM_BYTES = 64 * 1024 * 1024
HEAD_PACK = LANES
V_PACK = V_HEAD + BF16_SUBLANES
TOKEN_TILE = 512
TOKEN_SUBTILES = 2
OUTPUT_TILE = 2048
OUTPUT_ROW_CHUNK = 256
CONV_CHUNK = 128
N_CONV_CHUNKS = E_A // CONV_CHUNK
CARRY_ROWS = SUBLANES
ATT_BLOCK = 512
ATT_QCHUNK = MXU_TILE
HEADS_PER_STEP = 4
ATT_FUSED_BLOCKS = 3
ATT_KV_UNROLL = 2
QK_LOOKAHEAD = 2
MASK_VALUE = float(jnp.finfo(jnp.float32).min)
VMEM_LIMIT_BYTES = V7X_VMEM_BYTES * 7 // 8

BF16 = jnp.bfloat16
F32 = jnp.float32


def _rms_scale(x):
    return lax.rsqrt(jnp.mean(x * x, axis=-1, keepdims=True) + EPS)


def _dot(a, b):
    return jnp.dot(a, b, preferred_element_type=F32)


_NT_DIMS = (((1,), (1,)), ((), ()))


def _token_kernel(x_ref, pos_ref, inv_freq_ref, win_ref, conv_ref, wouta_ref,
                  wdkv_ref, wkn_ref, wvt_ref, wcq_ref, wq_ref,
                  x1_ref, q_ref, k_ref, vt_ref,
                  carry_ref, y_ref):
    @pl.when(pl.program_id(1) == 0)
    def _():
        carry_ref[...] = jnp.zeros((CARRY_ROWS, E_A), F32)

    xn = [_conv_mixer_subtile(sub, x_ref, win_ref, conv_ref, wouta_ref, x1_ref, carry_ref, y_ref)
          for sub in range(TOKEN_SUBTILES)]
    _projections(xn, pos_ref, inv_freq_ref, wdkv_ref, wkn_ref, wvt_ref, wcq_ref, wq_ref, q_ref, k_ref, vt_ref)


def _conv_mixer_subtile(sub, x_ref, win_ref, conv_ref, wouta_ref, x1_ref, carry_ref, y_ref):
    tm = TOKEN_TILE
    cw = CONV_CHUNK
    rows = slice(sub * tm, (sub + 1) * tm)
    x = x_ref[rows, :]
    h = (x * _rms_scale(x)).astype(BF16)

    for j in range(N_CONV_CHUNKS):
        cols = slice(j * cw, (j + 1) * cw)
        pj = _dot(h, win_ref[:, j * 4 * cw:(j + 1) * 4 * cw])
        b = pj[:, 0:cw]
        c = pj[:, cw:2 * cw]
        u = pj[:, 2 * cw:3 * cw]
        g = pj[:, 3 * cw:4 * cw]
        v = c * u
        vv = jnp.concatenate([carry_ref[:, cols], v], axis=0)
        carry_ref[:, cols] = v[tm - CARRY_ROWS:tm]
        conv = None
        for k in range(CONV_WIDTH):
            shift = CONV_WIDTH - 1 - k
            tap = conv_ref[k:k + 1, cols] * (pltpu.roll(vv, shift, axis=0)[CARRY_ROWS:] if shift else v)
            conv = tap if conv is None else conv + tap
        y = (g * jax.nn.sigmoid(g)) * b * conv
        y_ref[rows, cols] = y.astype(BF16)

    x1 = x + _dot(y_ref[rows, :], wouta_ref[...])
    x1_ref[rows, :] = x1
    return (x1 * _rms_scale(x1)).astype(BF16)


def _projections(xn, pos_ref, inv_freq_ref, wdkv_ref, wkn_ref, wvt_ref, wcq_ref, wq_ref, q_ref, k_ref, vt_ref):
    tm = TOKEN_TILE
    subs = range(len(xn))
    rows = [slice(s * tm, (s + 1) * tm) for s in subs]
    lane = lax.broadcasted_iota(jnp.int32, (1, LANES), 1)
    low_half = lane < LANES // 2

    reps = LANES // QK_ROPE
    cc, ss = [], []
    for r in rows:
        ang = inv_freq_ref[...] * pos_ref[:, r].astype(F32)
        cos, sin = jnp.cos(ang), jnp.sin(ang)
        cc.append(jnp.concatenate([cos, cos] * reps, axis=0).T)
        ss.append(jnp.concatenate([-sin, sin] * reps, axis=0).T)

    ckr = [_dot(xn[s], wdkv_ref[...]) for s in subs]
    cq = [_dot(xn[s], wcq_ref[...]) for s in subs]
    rope_terms = [ckr[s][:, KV_RANK:KV_RANK + LANES] * jnp.where(low_half, cc[s], ss[s]) for s in subs]
    rope_k = [t + pltpu.roll(t, LANES // 2, axis=1) for t in rope_terms]
    ckvn = [(ckr[s][:, 0:KV_RANK] * _rms_scale(ckr[s][:, 0:KV_RANK])).astype(BF16) for s in subs]
    cqn = [(c * _rms_scale(c)).astype(BF16) for c in cq]
    kn = [_dot(c, wkn_ref[...]) for c in ckvn]
    q = [_dot(c, wq_ref[...]) for c in cqn]
    head_row = lax.broadcasted_iota(jnp.int32, (N_HEADS * V_PACK, 1), 0) % V_PACK
    ones_row = jnp.where(head_row == V_HEAD, 1.0, 0.0)
    for s in subs:
        vt = lax.dot_general(wvt_ref[...], ckvn[s], _NT_DIMS, preferred_element_type=F32)
        vt_ref[s] = (vt + ones_row).astype(BF16)

    qscale = SOFTMAX_SCALE * math.log2(math.e)
    for s in subs:
        tq = jnp.where(lane < QK_NOPE, 1.0, jnp.where(lane < QK_NOPE + QK_ROPE, cc[s], ss[s])) * qscale
        for hd in range(N_HEADS):
            hc = slice(hd * HEAD_PACK, (hd + 1) * HEAD_PACK)
            kn_pair = kn[s][:, (hd // 2) * LANES:(hd // 2 + 1) * LANES]
            if hd % 2 == 1:
                kn_pair = pltpu.roll(kn_pair, LANES // 2, axis=1)
            k_ref[rows[s], hc] = jnp.where(low_half, kn_pair, rope_k[s]).astype(BF16)
            q_ref[rows[s], hc] = (q[s][:, hc] * tq).astype(BF16)


def _attention_kernel(q_ref, k_ref, vt_ref, o_ref, m_ref, alpha_ref, acc_ref, *st_refs):
    blk = ATT_BLOCK
    cw = ATT_QCHUNK
    nblk = q_ref.shape[0] // blk
    chains = [(hd, c) for hd in range(HEADS_PER_STEP) for c in range(blk // cw)]
    assert len(st_refs) == len(chains)

    def head_cols(i):
        hd = chains[i][0]
        return slice(hd * HEAD_PACK, (hd + 1) * HEAD_PACK)

    def state(ref, i):
        hd, c = chains[i]
        return ref.at[hd, :, c * cw:(c + 1) * cw]

    def qk(i, qi, ki, diagonal):
        c = chains[i][1]
        rows = (c + 1) * cw if diagonal else blk
        q = q_ref[pl.ds(pl.multiple_of(qi * blk + c * cw, cw), cw), head_cols(i)]
        k = k_ref[pl.ds(pl.multiple_of(ki * blk, blk), rows), head_cols(i)]
        return lax.dot_general(k, q, _NT_DIMS, preferred_element_type=F32)

    def stage_scores(i, s, diagonal):
        st = st_refs[i]
        if diagonal:
            c = chains[i][1]
            lo, hi = c * cw, (c + 1) * cw
            kv_idx = lax.broadcasted_iota(jnp.int32, (cw, cw), 0)
            q_idx = lax.broadcasted_iota(jnp.int32, (cw, cw), 1)
            tri = jnp.where(kv_idx <= q_idx, s[lo:hi], MASK_VALUE)
            m_new = jnp.max(tri, axis=0, keepdims=True)
            st[lo:hi, :] = tri
            if lo > 0:
                m_new = jnp.maximum(m_new, jnp.max(s[0:lo], axis=0, keepdims=True))
                st[0:lo, :] = s[0:lo]
            if hi < blk:
                st[hi:blk, :] = jnp.full((blk - hi, cw), MASK_VALUE, F32)
            m_old = jnp.full((1, cw), MASK_VALUE, F32)
        else:
            m_old = state(m_ref, i)[...]
            m_new = jnp.maximum(m_old, jnp.max(s, axis=0, keepdims=True))
            st[...] = s
        state(alpha_ref, i)[...] = jnp.exp2(m_old - m_new)
        state(m_ref, i)[...] = m_new

    def accumulate(i, ki, staged_diagonal=False):
        hd, c = chains[i]
        rows = (c + 1) * cw if staged_diagonal else blk
        vt = vt_ref[ki, hd * V_PACK:(hd + 1) * V_PACK, 0:rows]
        p = jnp.exp2(st_refs[i][0:rows, :] - state(m_ref, i)[...]).astype(BF16)
        acc = state(acc_ref, i)
        acc[...] = state(alpha_ref, i)[...] * acc[...] + _dot(vt, p)

    def step(qi, ki, diagonal, staged_ki, staged_diagonal=False):
        s = {j: qk(j, qi, ki, diagonal) for j in range(QK_LOOKAHEAD)}
        for i in range(len(chains)):
            if i + QK_LOOKAHEAD < len(chains):
                s[i + QK_LOOKAHEAD] = qk(i + QK_LOOKAHEAD, qi, ki, diagonal)
            accumulate(i, staged_ki, staged_diagonal)
            stage_scores(i, s.pop(i), diagonal)

    def finalize(qi):
        o_t = jnp.concatenate([acc_ref[hd, 0:V_HEAD, :] / acc_ref[hd, V_HEAD:V_HEAD + 1, :]
                               for hd in range(HEADS_PER_STEP)], axis=0)
        o_ref[pl.ds(pl.multiple_of(qi * blk, blk), blk), :] = o_t.T.astype(o_ref.dtype)

    acc_ref[...] = jnp.zeros(acc_ref.shape, F32)
    for i in range(len(chains)):
        stage_scores(i, qk(i, 0, 0, True), True)

    def q_block(qi, staged_ki, fused):
        step(qi, qi, True, staged_ki)
        finalize(qi - 1)
        step(qi, 0, False, qi, staged_diagonal=True)
        for ki in range(1, fused):
            step(qi, ki, False, ki - 1)

        unroll = ATT_KV_UNROLL

        def kv_group(j, staged):
            for u in range(unroll):
                ki = unroll * j + u + fused
                step(qi, ki, False, staged)
                staged = ki
            return staged

        n_groups = lax.shift_right_logical(qi - fused, unroll.bit_length() - 1)
        staged = lax.fori_loop(0, n_groups, kv_group, fused - 1)

        def kv_single(ki, staged):
            step(qi, ki, False, staged)
            return ki

        return lax.fori_loop(n_groups * unroll + fused, qi, kv_single, staged)

    staged_ki = 0
    for qi in range(1, ATT_FUSED_BLOCKS):
        staged_ki = q_block(jnp.int32(qi), staged_ki, qi)
    staged_ki = lax.fori_loop(ATT_FUSED_BLOCKS, nblk, partial(q_block, fused=ATT_FUSED_BLOCKS), staged_ki)
    for i in range(len(chains)):
        accumulate(i, staged_ki)
    finalize(nblk - 1)


def _output_kernel(x1_ref, o_ref, wg_ref, woutb_ref, fnorm_ref, out_ref):
    rc = OUTPUT_ROW_CHUNK
    n_chunks = x1_ref.shape[0] // rc
    rows = lambda c: slice(c * rc, (c + 1) * rc)
    hb, g, y = {}, {}, {}

    def norm_in(c):
        x1 = x1_ref[rows(c), :]
        hb[c] = (x1 * _rms_scale(x1)).astype(BF16)

    def gate_proj(c):
        g[c] = _dot(hb.pop(c), wg_ref[...])

    def gate(c):
        gc = g.pop(c)
        y[c] = (o_ref[rows(c), :].astype(F32) * (gc * jax.nn.sigmoid(gc))).astype(BF16)

    def out_proj_norm(c):
        x2 = x1_ref[rows(c), :] + _dot(y.pop(c), woutb_ref[...])
        out_ref[rows(c), :] = x2 * _rms_scale(x2) * fnorm_ref[...]

    stages = (norm_in, gate_proj, gate, out_proj_norm)
    for t in range(n_chunks + len(stages) - 1):
        for depth, stage in reversed(list(enumerate(stages))):
            if 0 <= t - depth < n_chunks:
                stage(t - depth)


def _const_spec(shape):
    return pl.BlockSpec(shape, lambda *_: (0,) * len(shape))


def _swap_halves(w):
    half = w.shape[-1] // 2
    return jnp.concatenate([w[..., half:], w[..., :half]], axis=-1)


def _prepare_weights(a_norm, a_w_in, kv_norm, w_dkv, ckv_norm, w_ukv, b_norm, b_w_in, q_norm, b_w_uq):
    rows = lambda g: g.astype(F32)[:, None]

    a_w_in = rows(a_norm) * a_w_in
    win = jnp.concatenate(
        [a_w_in[:, part * E_A + j * CONV_CHUNK:part * E_A + (j + 1) * CONV_CHUNK]
         for j in range(N_CONV_CHUNKS) for part in range(4)], axis=-1).astype(BF16)

    w_dkv = rows(kv_norm) * w_dkv
    kr = w_dkv[:, KV_RANK:]
    wdkv = jnp.concatenate([w_dkv[:, :KV_RANK], kr, kr, _swap_halves(kr), _swap_halves(kr)],
                           axis=-1).astype(BF16)

    wk = (rows(ckv_norm) * w_ukv).reshape(KV_RANK, N_HEADS, QK_NOPE + V_HEAD)
    wkn = wk[..., :QK_NOPE].reshape(KV_RANK, N_HEADS * QK_NOPE).astype(BF16)
    pad_v = jnp.zeros((KV_RANK, N_HEADS, V_PACK - V_HEAD), F32)
    wvt = jnp.concatenate([wk[..., QK_NOPE:], pad_v], axis=-1).reshape(KV_RANK, N_HEADS * V_PACK)
    wvt = wvt.T.astype(BF16)

    b_w_in = rows(b_norm) * b_w_in
    wcq = b_w_in[:, :Q_RANK].astype(BF16)
    wg = b_w_in[:, Q_RANK:].astype(BF16)

    wq = (rows(q_norm) * b_w_uq).reshape(Q_RANK, N_HEADS, QK_NOPE + QK_ROPE)
    rope = wq[..., QK_NOPE:]
    wq = jnp.concatenate([wq[..., :QK_NOPE], rope, _swap_halves(rope)], axis=-1)
    wq = wq.reshape(Q_RANK, N_HEADS * HEAD_PACK).astype(BF16)
    return win, wdkv, wkn, wvt, wcq, wg, wq


def kernel(x, positions, a_norm, a_w_in, a_conv, a_w_out, kv_norm, w_dkv, ckv_norm, w_ukv,
           b_norm, b_w_in, b_q_norm, b_w_uq, b_w_out, final_norm):
    bsz, seq, d = x.shape
    assert d == D_MODEL and seq % (TOKEN_SUBTILES * TOKEN_TILE) == 0
    assert seq % ATT_BLOCK == 0 and seq % OUTPUT_TILE == 0
    assert a_norm.shape[0] == 1 and b_norm.shape[0] == 1, "depth-2 trunk: one conv layer, one MLA layer"
    assert a_conv.shape[1:] == (CONV_WIDTH, E_A) and CONV_WIDTH - 1 <= CARRY_ROWS

    inv_freq = ROPE_THETA ** (-jnp.arange(0, QK_ROPE, 2, dtype=F32) / QK_ROPE)
    inv_freq = inv_freq.reshape(QK_ROPE // 2, 1)
    pos = positions.reshape(bsz, 1, seq)
    win, wdkv, wkn, wvt, wcq, wg, wq = _prepare_weights(
        a_norm[0], a_w_in[0], kv_norm, w_dkv, ckv_norm, w_ukv, b_norm[0], b_w_in[0], b_q_norm[0], b_w_uq[0])
    wouta = a_w_out[0].astype(BF16)
    woutb = b_w_out[0].astype(BF16)

    tm = TOKEN_TILE
    assert tm == ATT_BLOCK, "v'^T is written one attention kv block per token sub-tile"
    n_tiles = seq // tm
    step_rows = TOKEN_SUBTILES * tm
    hp = N_HEADS * HEAD_PACK
    tok_spec = lambda width: pl.BlockSpec((None, step_rows, width), lambda b, s: (b, s, 0))
    pos_spec = pl.BlockSpec((None, 1, step_rows), lambda b, s: (b, 0, s))

    x1, q, k, vt = pl.pallas_call(
        _token_kernel,
        grid=(bsz, seq // step_rows),
        in_specs=[
            tok_spec(D_MODEL), pos_spec, _const_spec(inv_freq.shape),
            _const_spec(win.shape), _const_spec(a_conv.shape[1:]), _const_spec(wouta.shape),
            _const_spec(wdkv.shape), _const_spec(wkn.shape), _const_spec(wvt.shape),
            _const_spec(wcq.shape), _const_spec(wq.shape),
        ],
        out_specs=[tok_spec(D_MODEL), tok_spec(hp), tok_spec(hp),
                   pl.BlockSpec((None, TOKEN_SUBTILES, N_HEADS * V_PACK, tm),
                                lambda b, s: (b, s, 0, 0))],
        out_shape=[
            jax.ShapeDtypeStruct((bsz, seq, D_MODEL), F32),
            jax.ShapeDtypeStruct((bsz, seq, hp), BF16),
            jax.ShapeDtypeStruct((bsz, seq, hp), BF16),
            jax.ShapeDtypeStruct((bsz, n_tiles, N_HEADS * V_PACK, tm), BF16),
        ],
        scratch_shapes=[
            pltpu.VMEM((CARRY_ROWS, E_A), F32),
            pltpu.VMEM((step_rows, E_A), BF16),
        ],
        compiler_params=pltpu.CompilerParams(
            dimension_semantics=("arbitrary", "arbitrary"),
            vmem_limit_bytes=VMEM_LIMIT_BYTES),
        name="token_kernel",
    )(x, pos, inv_freq, win, a_conv[0], wouta, wdkv, wkn, wvt, wcq, wq)

    hw = HEADS_PER_STEP * HEAD_PACK
    att_spec = pl.BlockSpec((None, seq, hw), lambda b, h: (b, 0, h))
    o = pl.pallas_call(
        _attention_kernel,
        grid=(bsz, N_HEADS // HEADS_PER_STEP),
        in_specs=[att_spec, att_spec,
                  pl.BlockSpec((None, n_tiles, HEADS_PER_STEP * V_PACK, ATT_BLOCK),
                               lambda b, h: (b, 0, h, 0))],
        out_specs=pl.BlockSpec((None, seq, HEADS_PER_STEP * V_HEAD), lambda b, h: (b, 0, h)),
        out_shape=jax.ShapeDtypeStruct((bsz, seq, E_B), BF16),
        scratch_shapes=[
            pltpu.VMEM((HEADS_PER_STEP, 1, ATT_BLOCK), F32),
            pltpu.VMEM((HEADS_PER_STEP, 1, ATT_BLOCK), F32),
            pltpu.VMEM((HEADS_PER_STEP, V_PACK, ATT_BLOCK), F32),
        ] + [pltpu.VMEM((ATT_BLOCK, ATT_QCHUNK), F32)
             for _ in range(HEADS_PER_STEP * (ATT_BLOCK // ATT_QCHUNK))],
        compiler_params=pltpu.CompilerParams(
            dimension_semantics=("arbitrary", "arbitrary"),
            vmem_limit_bytes=VMEM_LIMIT_BYTES),
        name="attention_kernel",
    )(q, k, vt)

    out_spec = lambda width: pl.BlockSpec((None, OUTPUT_TILE, width), lambda b, s: (b, s, 0))
    out = pl.pallas_call(
        _output_kernel,
        grid=(bsz, seq // OUTPUT_TILE),
        in_specs=[
            out_spec(D_MODEL), out_spec(E_B), _const_spec(wg.shape), _const_spec(woutb.shape),
            _const_spec((1, D_MODEL)),
        ],
        out_specs=out_spec(D_MODEL),
        out_shape=jax.ShapeDtypeStruct((bsz, seq, D_MODEL), F32),
        compiler_params=pltpu.CompilerParams(
            dimension_semantics=("arbitrary", "arbitrary"),
            vmem_limit_bytes=VMEM_LIMIT_BYTES),
        name="output_kernel",
    )(x1, o, wg, woutb, final_norm.reshape(1, D_MODEL).astype(F32))
    return out
```

```python
import math
from functools import partial

import jax
import jax.numpy as jnp
from jax import lax
from jax.experimental import pallas as pl
from jax.experimental.pallas import tpu as pltpu

D_MODEL = 1024
E_A = D_MODEL
CONV_WIDTH = 3
N_HEADS = 8
QK_NOPE = 64
QK_ROPE = 32
V_HEAD = 64
KV_RANK = 256
Q_RANK = 384
E_B = N_HEADS * V_HEAD
ROPE_THETA = 10000.0
SOFTMAX_SCALE = 1.0 / math.sqrt(QK_NOPE + QK_ROPE)
EPS = 1e-6

LANES = 128
SUBLANES = 8
BF16_SUBLANES = 2 * SUBLANES
MXU_TILE = 256
V7X_VMEM_BYTES = 64 * 1024 * 1024
HEAD_PACK = LANES
V_PACK = V_HEAD + BF16_SUBLANES
TOKEN_TILE = 512
TOKEN_SUBTILES = 2
OUTPUT_TILE = 2048
OUTPUT_ROW_CHUNK = 256
CONV_CHUNK = 128
N_CONV_CHUNKS = E_A // CONV_CHUNK
CARRY_ROWS = SUBLANES
ATT_BLOCK = 512
ATT_QCHUNK = MXU_TILE
HEADS_PER_STEP = 4
ATT_FUSED_BLOCKS = 4
ATT_KV_UNROLL = 2
QK_LOOKAHEAD = 2
MASK_VALUE = float(jnp.finfo(jnp.float32).min)
VMEM_LIMIT_BYTES = V7X_VMEM_BYTES * 7 // 8

BF16 = jnp.bfloat16
F32 = jnp.float32


def _rms_scale(x):
    return lax.rsqrt(jnp.mean(x * x, axis=-1, keepdims=True) + EPS)


def _dot(a, b):
    return jnp.dot(a, b, preferred_element_type=F32)


_NT_DIMS = (((1,), (1,)), ((), ()))


def _token_kernel(x_ref, pos_ref, inv_freq_ref, win_ref, conv_ref, wouta_ref,
                  wdkv_ref, wkn_ref, wvt_ref, wcq_ref, wq_ref,
                  x1_ref, q_ref, k_ref, vt_ref,
                  carry_ref, y_ref):
    @pl.when(pl.program_id(1) == 0)
    def _():
        carry_ref[...] = jnp.zeros((CARRY_ROWS, E_A), F32)

    xn = [_conv_mixer_subtile(sub, x_ref, win_ref, conv_ref, wouta_ref, x1_ref, carry_ref, y_ref)
          for sub in range(TOKEN_SUBTILES)]
    _projections(xn, pos_ref, inv_freq_ref, wdkv_ref, wkn_ref, wvt_ref, wcq_ref, wq_ref, q_ref, k_ref, vt_ref)


def _conv_mixer_subtile(sub, x_ref, win_ref, conv_ref, wouta_ref, x1_ref, carry_ref, y_ref):
    tm = TOKEN_TILE
    cw = CONV_CHUNK
    rows = slice(sub * tm, (sub + 1) * tm)
    x = x_ref[rows, :]
    h = (x * _rms_scale(x)).astype(BF16)

    for j in range(N_CONV_CHUNKS):
        cols = slice(j * cw, (j + 1) * cw)
        pj = _dot(h, win_ref[:, j * 4 * cw:(j + 1) * 4 * cw])
        b = pj[:, 0:cw]
        c = pj[:, cw:2 * cw]
        u = pj[:, 2 * cw:3 * cw]
        g = pj[:, 3 * cw:4 * cw]
        v = c * u
        vv = jnp.concatenate([carry_ref[:, cols], v], axis=0)
        carry_ref[:, cols] = v[tm - CARRY_ROWS:tm]
        conv = None
        for k in range(CONV_WIDTH):
            shift = CONV_WIDTH - 1 - k
            tap = conv_ref[k:k + 1, cols] * (pltpu.roll(vv, shift, axis=0)[CARRY_ROWS:] if shift else v)
            conv = tap if conv is None else conv + tap
        y = (g * jax.nn.sigmoid(g)) * b * conv
        y_ref[rows, cols] = y.astype(BF16)

    x1 = x + _dot(y_ref[rows, :], wouta_ref[...])
    x1_ref[rows, :] = x1
    return (x1 * _rms_scale(x1)).astype(BF16)


def _projections(xn, pos_ref, inv_freq_ref, wdkv_ref, wkn_ref, wvt_ref, wcq_ref, wq_ref, q_ref, k_ref, vt_ref):
    tm = TOKEN_TILE
    subs = range(len(xn))
    rows = [slice(s * tm, (s + 1) * tm) for s in subs]
    lane = lax.broadcasted_iota(jnp.int32, (1, LANES), 1)
    low_half = lane < LANES // 2

    reps = LANES // QK_ROPE
    cc, ss = [], []
    for r in rows:
        ang = inv_freq_ref[...] * pos_ref[:, r].astype(F32)
        cos, sin = jnp.cos(ang), jnp.sin(ang)
        cc.append(jnp.concatenate([cos, cos] * reps, axis=0).T)
        ss.append(jnp.concatenate([-sin, sin] * reps, axis=0).T)

    ckr = [_dot(xn[s], wdkv_ref[...]) for s in subs]
    cq = [_dot(xn[s], wcq_ref[...]) for s in subs]
    rope_terms = [ckr[s][:, KV_RANK:KV_RANK + LANES] * jnp.where(low_half, cc[s], ss[s]) for s in subs]
    rope_k = [t + pltpu.roll(t, LANES // 2, axis=1) for t in rope_terms]
    ckvn = [(ckr[s][:, 0:KV_RANK] * _rms_scale(ckr[s][:, 0:KV_RANK])).astype(BF16) for s in subs]
    cqn = [(c * _rms_scale(c)).astype(BF16) for c in cq]
    kn = [_dot(c, wkn_ref[...]) for c in ckvn]
    q = [_dot(c, wq_ref[...]) for c in cqn]
    head_row = lax.broadcasted_iota(jnp.int32, (N_HEADS * V_PACK, 1), 0) % V_PACK
    ones_row = jnp.where(head_row == V_HEAD, 1.0, 0.0)
    for s in subs:
        vt = lax.dot_general(wvt_ref[...], ckvn[s], _NT_DIMS, preferred_element_type=F32)
        vt_ref[s] = (vt + ones_row).astype(BF16)

    qscale = SOFTMAX_SCALE * math.log2(math.e)
    for s in subs:
        tq = jnp.where(lane < QK_NOPE, 1.0, jnp.where(lane < QK_NOPE + QK_ROPE, cc[s], ss[s])) * qscale
        for hd in range(N_HEADS):
            hc = slice(hd * HEAD_PACK, (hd + 1) * HEAD_PACK)
            kn_pair = kn[s][:, (hd // 2) * LANES:(hd // 2 + 1) * LANES]
            if hd % 2 == 1:
                kn_pair = pltpu.roll(kn_pair, LANES // 2, axis=1)
            k_ref[rows[s], hc] = jnp.where(low_half, kn_pair, rope_k[s]).astype(BF16)
            q_ref[rows[s], hc] = (q[s][:, hc] * tq).astype(BF16)


def _attention_kernel(q_ref, k_ref, vt_ref, o_ref, m_ref, alpha_ref, acc_ref, *st_refs):
    blk = ATT_BLOCK
    cw = ATT_QCHUNK
    nblk = q_ref.shape[0] // blk
    chains = [(hd, c) for hd in range(HEADS_PER_STEP) for c in range(blk // cw)]
    assert len(st_refs) == len(chains)

    def head_cols(i):
        hd = chains[i][0]
        return slice(hd * HEAD_PACK, (hd + 1) * HEAD_PACK)

    def state(ref, i):
        hd, c = chains[i]
        return ref.at[hd, :, c * cw:(c + 1) * cw]

    def qk(i, qi, ki, diagonal):
        c = chains[i][1]
        rows = (c + 1) * cw if diagonal else blk
        q = q_ref[pl.ds(pl.multiple_of(qi * blk + c * cw, cw), cw), head_cols(i)]
        k = k_ref[pl.ds(pl.multiple_of(ki * blk, blk), rows), head_cols(i)]
        return lax.dot_general(k, q, _NT_DIMS, preferred_element_type=F32)

    def stage_scores(i, s, diagonal):
        st = st_refs[i]
        if diagonal:
            c = chains[i][1]
            lo, hi = c * cw, (c + 1) * cw
            kv_idx = lax.broadcasted_iota(jnp.int32, (cw, cw), 0)
            q_idx = lax.broadcasted_iota(jnp.int32, (cw, cw), 1)
            tri = jnp.where(kv_idx <= q_idx, s[lo:hi], MASK_VALUE)
            m_new = jnp.max(tri, axis=0, keepdims=True)
            st[lo:hi, :] = tri
            if lo > 0:
                m_new = jnp.maximum(m_new, jnp.max(s[0:lo], axis=0, keepdims=True))
                st[0:lo, :] = s[0:lo]
            if hi < blk:
                st[hi:blk, :] = jnp.full((blk - hi, cw), MASK_VALUE, F32)
            m_old = jnp.full((1, cw), MASK_VALUE, F32)
        else:
            m_old = state(m_ref, i)[...]
            m_new = jnp.maximum(m_old, jnp.max(s, axis=0, keepdims=True))
            st[...] = s
        state(alpha_ref, i)[...] = jnp.exp2(m_old - m_new)
        state(m_ref, i)[...] = m_new

    def accumulate(i, ki, staged_diagonal=False):
        hd, c = chains[i]
        rows = (c + 1) * cw if staged_diagonal else blk
        vt = vt_ref[ki, hd * V_PACK:(hd + 1) * V_PACK, 0:rows]
        p = jnp.exp2(st_refs[i][0:rows, :] - state(m_ref, i)[...]).astype(BF16)
        acc = state(acc_ref, i)
        acc[...] = state(alpha_ref, i)[...] * acc[...] + _dot(vt, p)

    def step(qi, ki, diagonal, staged_ki, staged_diagonal=False):
        s = {j: qk(j, qi, ki, diagonal) for j in range(QK_LOOKAHEAD)}
        for i in range(len(chains)):
            if i + QK_LOOKAHEAD < len(chains):
                s[i + QK_LOOKAHEAD] = qk(i + QK_LOOKAHEAD, qi, ki, diagonal)
            accumulate(i, staged_ki, staged_diagonal)
            stage_scores(i, s.pop(i), diagonal)

    def finalize(qi):
        o_t = jnp.concatenate([acc_ref[hd, 0:V_HEAD, :] / acc_ref[hd, V_HEAD:V_HEAD + 1, :]
                               for hd in range(HEADS_PER_STEP)], axis=0)
        o_ref[pl.ds(pl.multiple_of(qi * blk, blk), blk), :] = o_t.T.astype(o_ref.dtype)

    acc_ref[...] = jnp.zeros(acc_ref.shape, F32)
    for i in range(len(chains)):
        stage_scores(i, qk(i, 0, 0, True), True)

    def q_block(qi, staged_ki, fused):
        step(qi, qi, True, staged_ki)
        finalize(qi - 1)
        step(qi, 0, False, qi, staged_diagonal=True)
        for ki in range(1, fused):
            step(qi, ki, False, ki - 1)

        unroll = ATT_KV_UNROLL

        def kv_group(j, staged):
            for u in range(unroll):
                ki = unroll * j + u + fused
                step(qi, ki, False, staged)
                staged = ki
            return staged

        n_groups = lax.shift_right_logical(qi - fused, unroll.bit_length() - 1)
        staged = lax.fori_loop(0, n_groups, kv_group, fused - 1)

        def kv_single(ki, staged):
            step(qi, ki, False, staged)
            return ki

        return lax.fori_loop(n_groups * unroll + fused, qi, kv_single, staged)

    staged_ki = 0
    for qi in range(1, ATT_FUSED_BLOCKS):
        staged_ki = q_block(jnp.int32(qi), staged_ki, qi)
    staged_ki = lax.fori_loop(ATT_FUSED_BLOCKS, nblk, partial(q_block, fused=ATT_FUSED_BLOCKS), staged_ki)
    for i in range(len(chains)):
        accumulate(i, staged_ki)
    finalize(nblk - 1)


def _output_kernel(x1_ref, o_ref, wg_ref, woutb_ref, fnorm_ref, out_ref):
    rc = OUTPUT_ROW_CHUNK
    n_chunks = x1_ref.shape[0] // rc
    rows = lambda c: slice(c * rc, (c + 1) * rc)
    hb, g, y = {}, {}, {}

    def norm_in(c):
        x1 = x1_ref[rows(c), :]
        hb[c] = (x1 * _rms_scale(x1)).astype(BF16)

    def gate_proj(c):
        g[c] = _dot(hb.pop(c), wg_ref[...])

    def gate(c):
        gc = g.pop(c)
        y[c] = (o_ref[rows(c), :].astype(F32) * (gc * jax.nn.sigmoid(gc))).astype(BF16)

    def out_proj_norm(c):
        x2 = x1_ref[rows(c), :] + _dot(y.pop(c), woutb_ref[...])
        out_ref[rows(c), :] = x2 * _rms_scale(x2) * fnorm_ref[...]

    stages = (norm_in, gate_proj, gate, out_proj_norm)
    for t in range(n_chunks + len(stages) - 1):
        for depth, stage in reversed(list(enumerate(stages))):
            if 0 <= t - depth < n_chunks:
                stage(t - depth)


def _const_spec(shape):
    return pl.BlockSpec(shape, lambda *_: (0,) * len(shape))


def _swap_halves(w):
    half = w.shape[-1] // 2
    return jnp.concatenate([w[..., half:], w[..., :half]], axis=-1)


def _prepare_weights(a_norm, a_w_in, kv_norm, w_dkv, ckv_norm, w_ukv, b_norm, b_w_in, q_norm, b_w_uq):
    rows = lambda g: g.astype(F32)[:, None]

    a_w_in = rows(a_norm) * a_w_in
    win = jnp.concatenate(
        [a_w_in[:, part * E_A + j * CONV_CHUNK:part * E_A + (j + 1) * CONV_CHUNK]
         for j in range(N_CONV_CHUNKS) for part in range(4)], axis=-1).astype(BF16)

    w_dkv = rows(kv_norm) * w_dkv
    kr = w_dkv[:, KV_RANK:]
    wdkv = jnp.concatenate([w_dkv[:, :KV_RANK], kr, kr, _swap_halves(kr), _swap_halves(kr)],
                           axis=-1).astype(BF16)

    wk = (rows(ckv_norm) * w_ukv).reshape(KV_RANK, N_HEADS, QK_NOPE + V_HEAD)
    wkn = wk[..., :QK_NOPE].reshape(KV_RANK, N_HEADS * QK_NOPE).astype(BF16)
    pad_v = jnp.zeros((KV_RANK, N_HEADS, V_PACK - V_HEAD), F32)
    wvt = jnp.concatenate([wk[..., QK_NOPE:], pad_v], axis=-1).reshape(KV_RANK, N_HEADS * V_PACK)
    wvt = wvt.T.astype(BF16)

    b_w_in = rows(b_norm) * b_w_in
    wcq = b_w_in[:, :Q_RANK].astype(BF16)
    wg = b_w_in[:, Q_RANK:].astype(BF16)

    wq = (rows(q_norm) * b_w_uq).reshape(Q_RANK, N_HEADS, QK_NOPE + QK_ROPE)
    rope = wq[..., QK_NOPE:]
    wq = jnp.concatenate([wq[..., :QK_NOPE], rope, _swap_halves(rope)], axis=-1)
    wq = wq.reshape(Q_RANK, N_HEADS * HEAD_PACK).astype(BF16)
    return win, wdkv, wkn, wvt, wcq, wg, wq


def kernel(x, positions, a_norm, a_w_in, a_conv, a_w_out, kv_norm, w_dkv, ckv_norm, w_ukv,
           b_norm, b_w_in, b_q_norm, b_w_uq, b_w_out, final_norm):
    bsz, seq, d = x.shape
    assert d == D_MODEL and seq % (TOKEN_SUBTILES * TOKEN_TILE) == 0
    assert seq % ATT_BLOCK == 0 and seq % OUTPUT_TILE == 0
    assert a_norm.shape[0] == 1 and b_norm.shape[0] == 1, "depth-2 trunk: one conv layer, one MLA layer"
    assert a_conv.shape[1:] == (CONV_WIDTH, E_A) and CONV_WIDTH - 1 <= CARRY_ROWS

    inv_freq = ROPE_THETA ** (-jnp.arange(0, QK_ROPE, 2, dtype=F32) / QK_ROPE)
    inv_freq = inv_freq.reshape(QK_ROPE // 2, 1)
    pos = positions.reshape(bsz, 1, seq)
    win, wdkv, wkn, wvt, wcq, wg, wq = _prepare_weights(
        a_norm[0], a_w_in[0], kv_norm, w_dkv, ckv_norm, w_ukv, b_norm[0], b_w_in[0], b_q_norm[0], b_w_uq[0])
    wouta = a_w_out[0].astype(BF16)
    woutb = b_w_out[0].astype(BF16)

    tm = TOKEN_TILE
    assert tm == ATT_BLOCK, "v'^T is written one attention kv block per token sub-tile"
    n_tiles = seq // tm
    step_rows = TOKEN_SUBTILES * tm
    hp = N_HEADS * HEAD_PACK
    tok_spec = lambda width: pl.BlockSpec((None, step_rows, width), lambda b, s: (b, s, 0))
    pos_spec = pl.BlockSpec((None, 1, step_rows), lambda b, s: (b, 0, s))

    x1, q, k, vt = pl.pallas_call(
        _token_kernel,
        grid=(bsz, seq // step_rows),
        in_specs=[
            tok_spec(D_MODEL), pos_spec, _const_spec(inv_freq.shape),
            _const_spec(win.shape), _const_spec(a_conv.shape[1:]), _const_spec(wouta.shape),
            _const_spec(wdkv.shape), _const_spec(wkn.shape), _const_spec(wvt.shape),
            _const_spec(wcq.shape), _const_spec(wq.shape),
        ],
        out_specs=[tok_spec(D_MODEL), tok_spec(hp), tok_spec(hp),
                   pl.BlockSpec((None, TOKEN_SUBTILES, N_HEADS * V_PACK, tm),
                                lambda b, s: (b, s, 0, 0))],
        out_shape=[
            jax.ShapeDtypeStruct((bsz, seq, D_MODEL), F32),
            jax.ShapeDtypeStruct((bsz, seq, hp), BF16),
            jax.ShapeDtypeStruct((bsz, seq, hp), BF16),
            jax.ShapeDtypeStruct((bsz, n_tiles, N_HEADS * V_PACK, tm), BF16),
        ],
        scratch_shapes=[
            pltpu.VMEM((CARRY_ROWS, E_A), F32),
            pltpu.VMEM((step_rows, E_A), BF16),
        ],
        compiler_params=pltpu.CompilerParams(
            dimension_semantics=("arbitrary", "arbitrary"),
            vmem_limit_bytes=VMEM_LIMIT_BYTES),
        name="token_kernel",
    )(x, pos, inv_freq, win, a_conv[0], wouta, wdkv, wkn, wvt, wcq, wq)

    hw = HEADS_PER_STEP * HEAD_PACK
    att_spec = pl.BlockSpec((None, seq, hw), lambda b, h: (b, 0, h))
    o = pl.pallas_call(
        _attention_kernel,
        grid=(bsz, N_HEADS // HEADS_PER_STEP),
        in_specs=[att_spec, att_spec,
                  pl.BlockSpec((None, n_tiles, HEADS_PER_STEP * V_PACK, ATT_BLOCK),
                               lambda b, h: (b, 0, h, 0))],
        out_specs=pl.BlockSpec((None, seq, HEADS_PER_STEP * V_HEAD), lambda b, h: (b, 0, h)),
        out_shape=jax.ShapeDtypeStruct((bsz, seq, E_B), BF16),
        scratch_shapes=[
            pltpu.VMEM((HEADS_PER_STEP, 1, ATT_BLOCK), F32),
            pltpu.VMEM((HEADS_PER_STEP, 1, ATT_BLOCK), F32),
            pltpu.VMEM((HEADS_PER_STEP, V_PACK, ATT_BLOCK), F32),
        ] + [pltpu.VMEM((ATT_BLOCK, ATT_QCHUNK), F32)
             for _ in range(HEADS_PER_STEP * (ATT_BLOCK // ATT_QCHUNK))],
        compiler_params=pltpu.CompilerParams(
            dimension_semantics=("arbitrary", "arbitrary"),
            vmem_limit_bytes=VMEM_LIMIT_BYTES),
        name="attention_kernel",
    )(q, k, vt)

    out_spec = lambda width: pl.BlockSpec((None, OUTPUT_TILE, width), lambda b, s: (b, s, 0))
    out = pl.pallas_call(
        _output_kernel,
        grid=(bsz, seq // OUTPUT_TILE),
        in_specs=[
            out_spec(D_MODEL), out_spec(E_B), _const_spec(wg.shape), _const_spec(woutb.shape),
            _const_spec((1, D_MODEL)),
        ],
        out_specs=out_spec(D_MODEL),
        out_shape=jax.ShapeDtypeStruct((bsz, seq, D_MODEL), F32),
        compiler_params=pltpu.CompilerParams(
            dimension_semantics=("arbitrary", "arbitrary"),
            vmem_limit_bytes=VMEM_LIMIT_BYTES),
        name="output_kernel",
    )(x1, o, wg, woutb, final_norm.reshape(1, D_MODEL).astype(F32))
    return out
```
